```python
import math
import jax, jax.numpy as jnp
from jax import lax
import numpy as np

D_MODEL = 1024
BATCH = 32
SEQ = 2048
DEPTH = 1
DEC_BATCH = 128
DEC_SEQ = 8
PAST_LEN = 8192
PAGE_SIZE = 128

N_HEADS = 8
HEAD_DIM = 64
ATTN_W = N_HEADS * HEAD_DIM
SSM_W = 512
SSM_GROUP = 16
N_GROUPS = SSM_W // SSM_GROUP
STATE_DIM = 64
D_FF = 2816
Q_BLOCK = 128
EPS = 1e-6
Q0 = 0
K0 = ATTN_W
V0 = 2 * ATTN_W
F0 = 3 * ATTN_W
U0 = F0 + N_HEADS
GA0 = U0 + SSM_W
GB0 = GA0 + D_MODEL
IN_W = GB0 + D_MODEL

kernel_name = "fox_s5_gated_macaron_decode_step"


def rmsnorm(x, g):
    xf = x.astype(jnp.float32)
    xf = xf * lax.rsqrt(jnp.mean(xf * xf, axis=-1, keepdims=True) + EPS)
    return xf.astype(x.dtype) * g


def ffn_half(x, g, w_gate, w_up, w_down):
    h = rmsnorm(x, g)
    return x + 0.5 * ((jax.nn.silu(h @ w_gate) * (h @ w_up)) @ w_down)


def in_proj(h, w_in, b_f):
    n, l = h.shape[:2]
    z = h @ w_in
    q = z[..., Q0:K0].reshape(n, l, N_HEADS, HEAD_DIM)
    k = z[..., K0:V0].reshape(n, l, N_HEADS, HEAD_DIM)
    v = z[..., V0:F0].reshape(n, l, N_HEADS, HEAD_DIM)
    lf = jax.nn.log_sigmoid((z[..., F0:U0] + b_f).astype(jnp.float32))
    u = z[..., U0:GA0].reshape(n, l, N_GROUPS, SSM_GROUP)
    g_a = jax.nn.sigmoid(z[..., GA0:GB0])
    g_b = jax.nn.sigmoid(z[..., GB0:IN_W])
    return q, k, v, lf, u, g_a, g_b


def fox_prompt(q, k, v, lf):
    n, l = q.shape[:2]
    scale = HEAD_DIM ** -0.5
    c_t = jnp.cumsum(lf, axis=1).transpose(0, 2, 1)
    nb = l // Q_BLOCK
    qb = q.reshape(n, nb, Q_BLOCK, N_HEADS, HEAD_DIM).transpose(1, 0, 2, 3, 4)
    cb = c_t.reshape(n, N_HEADS, nb, Q_BLOCK).transpose(2, 0, 1, 3)
    kpos = jnp.arange(l)

    def block(args):
        i, qi, ci = args
        s = jnp.einsum('nqhd,nkhd->nhqk', qi, k).astype(jnp.float32) * scale
        s = s + ci[..., None] - c_t[:, :, None, :]
        qpos = i * Q_BLOCK + jnp.arange(Q_BLOCK)
        s = jnp.where(kpos[None, :] <= qpos[:, None], s, -jnp.inf)
        p = jax.nn.softmax(s, axis=-1).astype(v.dtype)
        return jnp.einsum('nhqk,nkhd->nqhd', p, v)

    o = lax.map(block, (jnp.arange(nb), qb, cb))
    return o.transpose(1, 0, 2, 3, 4).reshape(n, l, ATTN_W)


def fox_sample(q, k_new, v_new, lf_new, k_past, v_past, lf_past):
    n, l = q.shape[:2]
    scale = HEAD_DIM ** -0.5
    lf_past = lf_past.astype(jnp.float32)
    r_past = (lax.cumsum(lf_past, axis=1, reverse=True) - lf_past).transpose(0, 2, 1)
    c_new = jnp.cumsum(lf_new.astype(jnp.float32), axis=1).transpose(0, 2, 1)
    s_past = jnp.einsum('nqhd,nkhd->nhqk', q, k_past).astype(jnp.float32) * scale
    s_past = s_past + c_new[..., None] + r_past[:, :, None, :]
    s_new = jnp.einsum('nqhd,nkhd->nhqk', q, k_new).astype(jnp.float32) * scale
    s_new = s_new + c_new[..., None] - c_new[:, :, None, :]
    pos = jnp.arange(l)
    s_new = jnp.where(pos[None, :] <= pos[:, None], s_new, -jnp.inf)
    p = jax.nn.softmax(jnp.concatenate([s_past, s_new], axis=-1), axis=-1).astype(v_new.dtype)
    n_past = k_past.shape[1]
    o = (jnp.einsum('nhqk,nkhd->nqhd', p[..., :n_past], v_past)
         + jnp.einsum('nhqk,nkhd->nqhd', p[..., n_past:], v_new))
    return o.reshape(n, l, ATTN_W)


def s5_scan(u, a_re, a_im, log_dt, b_re, b_im, c_re, c_im, d_skip, h0_re, h0_im):
    l = u.shape[1]
    dt = jnp.exp(log_dt)[:, None]
    mag = jnp.exp(a_re * dt)
    abar_re = mag * jnp.cos(a_im * dt)
    abar_im = mag * jnp.sin(a_im * dt)
    den = a_re * a_re + a_im * a_im
    num_re = abar_re - 1.0
    coef_re = (num_re * a_re + abar_im * a_im) / den
    coef_im = (abar_im * a_re - num_re * a_im) / den
    bu_re = jnp.einsum('nlgc,gpc->nlgp', u, b_re)
    bu_im = jnp.einsum('nlgc,gpc->nlgp', u, b_im)
    bb_re = coef_re * bu_re - coef_im * bu_im
    bb_im = coef_re * bu_im + coef_im * bu_re
    if h0_re is not None:
        bb_re = bb_re.at[:, 0].add(abar_re * h0_re - abar_im * h0_im)
        bb_im = bb_im.at[:, 0].add(abar_re * h0_im + abar_im * h0_re)
    ar = jnp.broadcast_to(abar_re, (1, l) + abar_re.shape)
    ai = jnp.broadcast_to(abar_im, (1, l) + abar_im.shape)

    def combine(e1, e2):
        a1r, a1i, b1r, b1i = e1
        a2r, a2i, b2r, b2i = e2
        return (a1r * a2r - a1i * a2i,
                a1r * a2i + a1i * a2r,
                a2r * b1r - a2i * b1i + b2r,
                a2r * b1i + a2i * b1r + b2i)

    _, _, h_re, h_im = lax.associative_scan(combine, (ar, ai, bb_re, bb_im), axis=1)
    y = (jnp.einsum('nlgp,gcp->nlgc', h_re, c_re)
         - jnp.einsum('nlgp,gcp->nlgc', h_im, c_im)
         + d_skip * u)
    return y.reshape(u.shape[0], l, SSM_W), h_re[:, -1], h_im[:, -1]


def decoder_layer(x, attend, h0_re, h0_im, lw):
    x = ffn_half(x, lw['norm_ffn1'], lw['w_ffn1_gate'], lw['w_ffn1_up'], lw['w_ffn1_down'])
    h = rmsnorm(x, lw['norm_mix'])
    q, k, v, lf, u, g_a, g_b = in_proj(h, lw['w_in'], lw['b_forget'])
    y_attn = attend(q, k, v, lf) @ lw['w_attn_out']
    y_ssm, hT_re, hT_im = s5_scan(u, lw['ssm_a_re'], lw['ssm_a_im'], lw['ssm_log_dt'],
                                  lw['ssm_b_re'], lw['ssm_b_im'], lw['ssm_c_re'], lw['ssm_c_im'],
                                  lw['ssm_d'], h0_re, h0_im)
    z = jax.nn.gelu(y_ssm)
    y_glu = (z @ lw['w_glu_a']) * jax.nn.sigmoid(z @ lw['w_glu_b'])
    x = x + (g_a * y_attn + g_b * y_glu) @ lw['w_o']
    x = ffn_half(x, lw['norm_ffn2'], lw['w_ffn2_gate'], lw['w_ffn2_up'], lw['w_ffn2_down'])
    return x, k, v, lf, hT_re, hT_im


def setup_inputs(seed: int = 0) -> dict:
    key = jax.random.key(seed)
    ks = jax.random.split(key, 48)
    counter = [0]

    def nk():
        counter[0] += 1
        return ks[counter[0] - 1]

    def nrm(shape, scale):
        return scale * jax.random.normal(nk(), shape, jnp.float32)

    L = DEPTH
    n_pages = PAST_LEN // PAGE_SIZE
    n_phys = (DEC_BATCH * n_pages * 5) // 4
    x_prompt = nrm((BATCH, SEQ, D_MODEL), 1.0)
    x_sample = nrm((DEC_BATCH, DEC_SEQ, D_MODEL), 1.0)
    cache_k = nrm((L, n_phys, PAGE_SIZE, N_HEADS, HEAD_DIM), 1.0)
    cache_v = nrm((L, n_phys, PAGE_SIZE, N_HEADS, HEAD_DIM), 1.0)
    cache_logf = jax.nn.log_sigmoid(3.0 + nrm((L, n_phys, PAGE_SIZE, N_HEADS), 1.0))
    state_ssm_re = nrm((L, DEC_BATCH, N_GROUPS, STATE_DIM), 0.1)
    state_ssm_im = nrm((L, DEC_BATCH, N_GROUPS, STATE_DIM), 0.1)
    page_table = jax.random.permutation(nk(), n_phys)[:DEC_BATCH * n_pages]
    page_table = page_table.reshape(DEC_BATCH, n_pages).astype(jnp.int32)
    gain = lambda: 1.0 + nrm((L, D_MODEL), 0.02)
    return {
        "x_prompt": x_prompt,
        "x_sample": x_sample,
        "cache_k": cache_k,
        "cache_v": cache_v,
        "cache_logf": cache_logf,
        "state_ssm_re": state_ssm_re,
        "state_ssm_im": state_ssm_im,
        "page_table": page_table,
        "norm_ffn1": gain(),
        "w_ffn1_gate": nrm((L, D_MODEL, D_FF), D_MODEL ** -0.5),
        "w_ffn1_up": nrm((L, D_MODEL, D_FF), D_MODEL ** -0.5),
        "w_ffn1_down": nrm((L, D_FF, D_MODEL), D_FF ** -0.5),
        "norm_mix": gain(),
        "w_in": nrm((L, D_MODEL, IN_W), D_MODEL ** -0.5),
        "b_forget": jax.random.uniform(nk(), (L, N_HEADS), jnp.float32, 2.0, 5.0),
        "ssm_a_re": -0.5 * jnp.exp(nrm((L, N_GROUPS, STATE_DIM), 0.02)),
        "ssm_a_im": math.pi * jnp.arange(STATE_DIM, dtype=jnp.float32)[None, None, :] + nrm((L, N_GROUPS, STATE_DIM), 0.01),
        "ssm_log_dt": jax.random.uniform(nk(), (L, N_GROUPS), jnp.float32, math.log(1e-3), math.log(1e-1)),
        "ssm_b_re": nrm((L, N_GROUPS, STATE_DIM, SSM_GROUP), (2 * SSM_GROUP) ** -0.5),
        "ssm_b_im": nrm((L, N_GROUPS, STATE_DIM, SSM_GROUP), (2 * SSM_GROUP) ** -0.5),
        "ssm_c_re": nrm((L, N_GROUPS, SSM_GROUP, STATE_DIM), STATE_DIM ** -0.5),
        "ssm_c_im": nrm((L, N_GROUPS, SSM_GROUP, STATE_DIM), STATE_DIM ** -0.5),
        "ssm_d": nrm((L, N_GROUPS, SSM_GROUP), 1.0),
        "w_attn_out": nrm((L, ATTN_W, D_MODEL), ATTN_W ** -0.5),
        "w_glu_a": nrm((L, SSM_W, D_MODEL), SSM_W ** -0.5),
        "w_glu_b": nrm((L, SSM_W, D_MODEL), SSM_W ** -0.5),
        "w_o": nrm((L, D_MODEL, D_MODEL), D_MODEL ** -0.5),
        "norm_ffn2": gain(),
        "w_ffn2_gate": nrm((L, D_MODEL, D_FF), D_MODEL ** -0.5),
        "w_ffn2_up": nrm((L, D_MODEL, D_FF), D_MODEL ** -0.5),
        "w_ffn2_down": nrm((L, D_FF, D_MODEL), D_FF ** -0.5),
        "norm_final": 1.0 + nrm((D_MODEL,), 0.02),
    }


def reference(x_prompt, x_sample, cache_k, cache_v, cache_logf, state_ssm_re, state_ssm_im, page_table,
              norm_ffn1, w_ffn1_gate, w_ffn1_up, w_ffn1_down, norm_mix, w_in, b_forget,
              ssm_a_re, ssm_a_im, ssm_log_dt, ssm_b_re, ssm_b_im, ssm_c_re, ssm_c_im, ssm_d,
              w_attn_out, w_glu_a, w_glu_b, w_o,
              norm_ffn2, w_ffn2_gate, w_ffn2_up, w_ffn2_down, norm_final):
    n_pages = PAST_LEN // PAGE_SIZE
    xp, xs = x_prompt, x_sample
    kp_l, vp_l, fp_l, ks_l, vs_l, fs_l = [], [], [], [], [], []
    hrp_l, hip_l, hrs_l, his_l = [], [], [], []
    for l in range(DEPTH):
        lw = dict(norm_ffn1=norm_ffn1[l], w_ffn1_gate=w_ffn1_gate[l], w_ffn1_up=w_ffn1_up[l],
                  w_ffn1_down=w_ffn1_down[l], norm_mix=norm_mix[l], w_in=w_in[l], b_forget=b_forget[l],
                  ssm_a_re=ssm_a_re[l], ssm_a_im=ssm_a_im[l], ssm_log_dt=ssm_log_dt[l],
                  ssm_b_re=ssm_b_re[l], ssm_b_im=ssm_b_im[l], ssm_c_re=ssm_c_re[l], ssm_c_im=ssm_c_im[l],
                  ssm_d=ssm_d[l], w_attn_out=w_attn_out[l], w_glu_a=w_glu_a[l], w_glu_b=w_glu_b[l],
                  w_o=w_o[l], norm_ffn2=norm_ffn2[l], w_ffn2_gate=w_ffn2_gate[l],
                  w_ffn2_up=w_ffn2_up[l], w_ffn2_down=w_ffn2_down[l])
        xp, kp, vp, fp, hrp, hip = decoder_layer(xp, fox_prompt, None, None, lw)
        k_past = cache_k[l, page_table].reshape(DEC_BATCH, n_pages * PAGE_SIZE, N_HEADS, HEAD_DIM)
        v_past = cache_v[l, page_table].reshape(DEC_BATCH, n_pages * PAGE_SIZE, N_HEADS, HEAD_DIM)
        f_past = cache_logf[l, page_table].reshape(DEC_BATCH, n_pages * PAGE_SIZE, N_HEADS)
        attend_s = lambda q, k, v, lf, kpa=k_past, vpa=v_past, fpa=f_past: fox_sample(q, k, v, lf, kpa, vpa, fpa)
        xs, ks_, vs_, fs_, hrs, his = decoder_layer(xs, attend_s, state_ssm_re[l], state_ssm_im[l], lw)
        kp_l.append(kp); vp_l.append(vp); fp_l.append(fp)
        ks_l.append(ks_); vs_l.append(vs_); fs_l.append(fs_)
        hrp_l.append(hrp); hip_l.append(hip); hrs_l.append(hrs); his_l.append(his)
    y_prompt = rmsnorm(xp, norm_final)
    y_sample = rmsnorm(xs, norm_final)
    return (y_prompt, y_sample,
            jnp.stack(kp_l), jnp.stack(vp_l), jnp.stack(fp_l),
            jnp.stack(ks_l), jnp.stack(vs_l), jnp.stack(fs_l),
            jnp.stack(hrp_l), jnp.stack(hip_l), jnp.stack(hrs_l), jnp.stack(his_l))
```

```python
import functools
import math

import jax
import jax.numpy as jnp
from jax import lax
from jax.experimental import pallas as pl
from jax.experimental.pallas import tpu as pltpu

F32 = jnp.float32
BF16 = jnp.bfloat16

N_HEADS = 8
HEAD_DIM = 64
ATTN_W = N_HEADS * HEAD_DIM
SSM_GROUP = 16
STATE_DIM = 64
EPS = 1e-6
LANES = 128
V7X_VMEM_BYTES = 64 * 1024 * 1024
VMEM_CAP_BYTES = V7X_VMEM_BYTES - 8 * 1024 * 1024
NEG_BIG = -1e30
PAGES_PER_STEP = 8


def _vmem_limit(estimate_bytes):
    return int(min(VMEM_CAP_BYTES, max(32 * 1024 * 1024, estimate_bytes * 5 // 4)))


def _resident(shape):
    nd = len(shape)
    return pl.BlockSpec(shape, lambda *_: (0,) * nd, pipeline_mode=pl.Buffered(1))


def _rmsnorm(x, g):
    return x * lax.rsqrt(jnp.mean(x * x, axis=-1, keepdims=True) + EPS) * g


def _swiglu_half(x, g_ref, wg_ref, wu_ref, wd_ref):
    h = _rmsnorm(x, g_ref[...]).astype(BF16)

    def chunk(c, acc):
        gate = jnp.dot(h, wg_ref[c], preferred_element_type=F32)
        up = jnp.dot(h, wu_ref[c], preferred_element_type=F32)
        a = (gate * jax.nn.sigmoid(gate) * up).astype(BF16)
        return acc + jnp.dot(a, wd_ref[c], preferred_element_type=F32)

    acc = lax.fori_loop(0, wg_ref.shape[0], chunk, jnp.zeros(x.shape, F32))
    return x + 0.5 * acc


def _log_sigmoid(x):
    return jnp.minimum(x, 0.0) - jnp.log1p(jnp.exp(-jnp.abs(x)))


def _ffn1_inproj_kernel(x_ref, g1_ref, wg_ref, wu_ref, wd_ref, g2_ref, wqkv_ref, wfT_ref, bf_ref, wus_ref,
                        x1_ref, q_ref, kb_ref, vb_ref, kf_ref, vf_ref, lfT_ref, u_ref):
    x1 = _swiglu_half(x_ref[...], g1_ref, wg_ref, wu_ref, wd_ref)
    x1_ref[...] = x1
    h = _rmsnorm(x1, g2_ref[...]).astype(BF16)
    zqkv = jnp.dot(h, wqkv_ref[...], preferred_element_type=F32)
    q_ref[...] = (zqkv[:, :ATTN_W] * (HEAD_DIM ** -0.5)).astype(BF16)
    k = zqkv[:, ATTN_W:2 * ATTN_W]
    v = zqkv[:, 2 * ATTN_W:]
    kf_ref[...] = k
    vf_ref[...] = v
    kb_ref[...] = k.astype(BF16)
    vb_ref[...] = v.astype(BF16)
    u_ref[...] = jnp.dot(h, wus_ref[...], preferred_element_type=F32)
    zfT = lax.dot_general(wfT_ref[...], h, (((1,), (1,)), ((), ())), preferred_element_type=F32)
    lfT_ref[...] = _log_sigmoid(zfT[:N_HEADS] + bf_ref[...])


def _ffn1_inproj(x, g1, wg, wu, wd, g2, wqkv, wfT, bf, wus, *, tm):
    n, d = x.shape
    nc, _, fc = wg.shape
    ssm_w = wus.shape[1]
    tok = lambda w: pl.BlockSpec((tm, w), lambda i: (i, 0))
    weights = (g1, wg, wu, wd, g2, wqkv, wfT, bf, wus)
    est = (sum(w.size * w.dtype.itemsize for w in weights)
           + 2 * tm * (2 * d * 4 + 3 * ATTN_W * 2 + 2 * ATTN_W * 4 + ssm_w * 4)
           + tm * (3 * d * 4 + 2 * fc * 4 + 3 * ATTN_W * 4))
    return pl.pallas_call(
        _ffn1_inproj_kernel,
        grid=(n // tm,),
        in_specs=[tok(d)] + [_resident(w.shape) for w in weights],
        out_specs=[tok(d), tok(ATTN_W), tok(ATTN_W), tok(ATTN_W), tok(ATTN_W), tok(ATTN_W),
                   pl.BlockSpec((N_HEADS, tm), lambda i: (0, i)), tok(ssm_w)],
        out_shape=[jax.ShapeDtypeStruct((n, d), F32),
                   jax.ShapeDtypeStruct((n, ATTN_W), BF16),
                   jax.ShapeDtypeStruct((n, ATTN_W), BF16),
                   jax.ShapeDtypeStruct((n, ATTN_W), BF16),
                   jax.ShapeDtypeStruct((n, ATTN_W), F32),
                   jax.ShapeDtypeStruct((n, ATTN_W), F32),
                   jax.ShapeDtypeStruct((N_HEADS, n), F32),
                   jax.ShapeDtypeStruct((n, ssm_w), F32)],
        compiler_params=pltpu.CompilerParams(dimension_semantics=("arbitrary",),
                                             vmem_limit_bytes=_vmem_limit(est)),
        name="ffn1_inproj",
    )(x, *weights)


def _split3(x):
    rnd = lambda a: a.astype(BF16).astype(F32)
    hi = rnd(x)
    mid = rnd(x - hi)
    lo = rnd(x - hi - mid)
    return hi, mid, lo


def _dot_exact(x, w01):
    hi, mid, lo = _split3(x)
    f = lambda a: jnp.dot(a, w01, preferred_element_type=F32)
    return f(hi) + f(mid) + f(lo)


def _tri_ones(n, strict_suffix):
    r = lax.broadcasted_iota(jnp.int32, (n, 2 * n), 0)
    c = lax.broadcasted_iota(jnp.int32, (n, 2 * n), 1)
    tri = (r > c) if strict_suffix else (r <= c)
    return jnp.where((c >= n) | tri, 1.0, 0.0).astype(F32)


def _attn_prompt_kernel(q_ref, k_ref, v_ref, lfT_ref, o_ref, c_ref, *, tq):
    seq = q_ref.shape[0]
    nq = seq // tq
    tri = _tri_ones(LANES, strict_suffix=False)
    carry = jnp.zeros((N_HEADS, LANES), F32)
    for blk in range(seq // LANES):
        both = _dot_exact(lfT_ref[:, blk * LANES:(blk + 1) * LANES], tri)
        c_blk = both[:, :LANES] + carry
        carry = carry + both[:, LANES:]
        j, off = divmod(blk * LANES, tq)
        c_ref[j, :, off:off + LANES] = c_blk

    lane = lax.broadcasted_iota(jnp.int32, (tq, LANES), 1)
    row = lax.broadcasted_iota(jnp.int32, (tq, tq), 0)
    col = lax.broadcasted_iota(jnp.int32, (tq, tq), 1)
    first = lane < HEAD_DIM

    for hp in range(N_HEADS // 2):
        cols = slice(hp * LANES, (hp + 1) * LANES)

        def q_block(qi, _):
            r0 = pl.multiple_of(qi * tq, tq)
            q = q_ref[pl.ds(r0, tq), cols]
            qs = (jnp.where(first, q, 0), jnp.where(first, 0, q))

            def scores(qe, k, cj):
                s = lax.dot_general(qe, k, (((1,), (1,)), ((), ())), preferred_element_type=F32)
                return s - cj

            def update(state, s, v):
                m, l, acc = state
                m_new = jnp.maximum(m, jnp.max(s, axis=1, keepdims=True))
                alpha = jnp.exp(m - m_new)
                p = jnp.exp(s - m_new)
                l = alpha * l + jnp.sum(p, axis=1, keepdims=True)
                acc = alpha * acc + jnp.dot(p.astype(BF16), v, preferred_element_type=F32)
                return m_new, l, acc

            def kv_step(j, states):
                c0 = pl.multiple_of(j * tq, tq)
                k = k_ref[pl.ds(c0, tq), cols]
                v = v_ref[pl.ds(c0, tq), cols]
                cj = c_ref[j]
                return tuple(update(states[e], scores(qs[e], k, cj[2 * hp + e:2 * hp + e + 1, :]), v)
                             for e in range(2))

            init = (jnp.full((tq, 1), NEG_BIG, F32), jnp.zeros((tq, 1), F32), jnp.zeros((tq, LANES), F32))
            states = lax.fori_loop(0, qi, kv_step, (init, init))
            k = k_ref[pl.ds(r0, tq), cols]
            v = v_ref[pl.ds(r0, tq), cols]
            cj = c_ref[qi]
            outs = []
            for e in range(2):
                s = scores(qs[e], k, cj[2 * hp + e:2 * hp + e + 1, :])
                s = jnp.where(col <= row, s, NEG_BIG)
                _, l, acc = update(states[e], s, v)
                outs.append(acc / l)
            o_ref[pl.ds(r0, tq), cols] = jnp.where(first, outs[0], outs[1]).astype(o_ref.dtype)
            return 0

        lax.fori_loop(0, nq, q_block, 0)


def _attn_prompt(q, k, v, lfT, *, batch, tq):
    n = q.shape[0]
    seq = n // batch
    q3, k3, v3 = (a.reshape(batch, seq, ATTN_W) for a in (q, k, v))
    blk = pl.BlockSpec((None, seq, ATTN_W), lambda b: (b, 0, 0))
    est = 2 * 4 * seq * ATTN_W * 2 + N_HEADS * seq * 4 * 3 + 16 * tq * tq * 4
    out = pl.pallas_call(
        functools.partial(_attn_prompt_kernel, tq=tq),
        grid=(batch,),
        in_specs=[blk, blk, blk, pl.BlockSpec((N_HEADS, seq), lambda b: (0, b))],
        out_specs=blk,
        out_shape=jax.ShapeDtypeStruct((batch, seq, ATTN_W), BF16),
        scratch_shapes=[pltpu.VMEM((seq // tq, N_HEADS, tq), F32)],
        compiler_params=pltpu.CompilerParams(dimension_semantics=("arbitrary",),
                                             vmem_limit_bytes=_vmem_limit(est)),
        name="attn_prompt",
    )(q3, k3, v3, lfT)
    return out.reshape(n, ATTN_W)


def _attn_sample_kernel(pt_ref, q_ref, kn_ref, vn_ref, lfn_ref, *refs, n_new):
    del pt_ref
    pp = PAGES_PER_STEP
    k_refs, v_refs, lf_refs = refs[:pp], refs[pp:2 * pp], refs[2 * pp:3 * pp]
    o_ref = refs[3 * pp]
    qbd_ref, m_ref, l_ref, acc_ref, carry_ref, r_ref = refs[3 * pp + 1:]
    rows = N_HEADS * n_new
    step = pl.program_id(1)
    hd_row = lax.broadcasted_iota(jnp.int32, (rows, ATTN_W), 0) // n_new
    hd_col = lax.broadcasted_iota(jnp.int32, (rows, ATTN_W), 1) // HEAD_DIM
    tri = _tri_ones(LANES, strict_suffix=True)

    @pl.when(step == 0)
    def _():
        q = q_ref[...]
        qt = jnp.concatenate([q] * N_HEADS, axis=0)
        qbd_ref[...] = jnp.where(hd_row == hd_col, qt, 0.0).astype(BF16)
        m_ref[...] = jnp.full(m_ref.shape, NEG_BIG, F32)
        l_ref[...] = jnp.zeros(l_ref.shape, F32)
        acc_ref[...] = jnp.zeros(acc_ref.shape, F32)
        carry_ref[...] = jnp.zeros(carry_ref.shape, F32)

    qbd = qbd_ref[...]

    def scores(k):
        return lax.dot_general(qbd, k.astype(BF16), (((1,), (1,)), ((), ())), preferred_element_type=F32)

    def add_head_bias(s, slot):
        return jnp.concatenate(
            [s[h * n_new:(h + 1) * n_new, :] + r_ref[slot, h:h + 1, :] for h in range(N_HEADS)], axis=0)

    def online_update(s_list, v_list):
        m_old = m_ref[...]
        m_new = m_old
        for s in s_list:
            m_new = jnp.maximum(m_new, jnp.max(s, axis=1, keepdims=True))
        alpha = jnp.exp(m_old - m_new)
        l = alpha * l_ref[...]
        acc = alpha * acc_ref[...]
        for s, v in zip(s_list, v_list):
            p = jnp.exp(s - m_new)
            l = l + jnp.sum(p, axis=1, keepdims=True)
            acc = acc + jnp.dot(p.astype(BF16), v.astype(BF16), preferred_element_type=F32)
        m_ref[...] = m_new
        l_ref[...] = l
        acc_ref[...] = acc

    carry = carry_ref[...]
    for i in reversed(range(pp)):
        both = _dot_exact(lf_refs[i][...], tri)
        r_ref[i] = both[:, :LANES] + carry
        carry = carry + both[:, LANES:]
    carry_ref[...] = carry
    online_update([add_head_bias(scores(k_refs[i][...]), i) for i in range(pp)],
                  [v_refs[i][...] for i in range(pp)])

    @pl.when(step == pl.num_programs(1) - 1)
    def _():
        both = _dot_exact(lfn_ref[...], tri)
        r_ref[0] = both[:, :LANES] - both[:, LANES:]
        pad = jnp.zeros((LANES - n_new, ATTN_W), F32)
        kn = jnp.concatenate([kn_ref[...], pad], axis=0)
        vn = jnp.concatenate([vn_ref[...], pad], axis=0)
        s = add_head_bias(scores(kn), 0)
        qi = lax.broadcasted_iota(jnp.int32, (rows, LANES), 0) % n_new
        kj = lax.broadcasted_iota(jnp.int32, (rows, LANES), 1)
        s = jnp.where(kj <= qi, s, NEG_BIG)
        online_update([s], [vn])
        o = acc_ref[...] / l_ref[...]
        o = jnp.where(hd_row == hd_col, o, 0.0)
        out = o[:n_new]
        for h in range(1, N_HEADS):
            out = out + o[h * n_new:(h + 1) * n_new]
        o_ref[...] = out.astype(o_ref.dtype)


def _attn_sample(page_table, q, k_new, v_new, lf_new_pad, cache_k, cache_v, cache_lfT, *, n_new):
    nseq, n_pages = page_table.shape
    pp = PAGES_PER_STEP
    steps = n_pages // pp
    rows = N_HEADS * n_new
    q3 = q.reshape(nseq, n_new, ATTN_W).astype(F32)
    kn3 = k_new.reshape(nseq, n_new, ATTN_W)
    vn3 = v_new.reshape(nseq, n_new, ATTN_W)
    page = cache_k.shape[1]

    def paged(width_dims, i):
        def index_map(b, s, pt):
            return (pt[b * n_pages + n_pages - pp * (s + 1) + i], 0, 0)
        return pl.BlockSpec((None,) + width_dims, index_map)

    per_seq = lambda r, w: pl.BlockSpec((None, r, w), lambda b, s, pt: (b, 0, 0))
    in_specs = ([per_seq(n_new, ATTN_W)] * 3 + [per_seq(N_HEADS, LANES)]
                + [paged((page, ATTN_W), i) for i in range(pp)]
                + [paged((page, ATTN_W), i) for i in range(pp)]
                + [paged((N_HEADS, page), i) for i in range(pp)])
    est = 2 * 2 * pp * page * ATTN_W * 4 + 8 * rows * ATTN_W * 4 + 4 * pp * rows * page * 4
    out = pl.pallas_call(
        functools.partial(_attn_sample_kernel, n_new=n_new),
        grid_spec=pltpu.PrefetchScalarGridSpec(
            num_scalar_prefetch=1,
            grid=(nseq, steps),
            in_specs=in_specs,
            out_specs=per_seq(n_new, ATTN_W),
            scratch_shapes=[pltpu.VMEM((rows, ATTN_W), BF16),
                            pltpu.VMEM((rows, 1), F32),
                            pltpu.VMEM((rows, 1), F32),
                            pltpu.VMEM((rows, ATTN_W), F32),
                            pltpu.VMEM((N_HEADS, LANES), F32),
                            pltpu.VMEM((pp, N_HEADS, LANES), F32)]),
        out_shape=jax.ShapeDtypeStruct((nseq, n_new, ATTN_W), BF16),
        compiler_params=pltpu.CompilerParams(dimension_semantics=("arbitrary", "arbitrary"),
                                             vmem_limit_bytes=_vmem_limit(est)),
        name="attn_sample",
    )(page_table.reshape(-1), q3, kn3, vn3, lf_new_pad,
      *([cache_k] * pp), *([cache_v] * pp), *([cache_lfT] * pp))
    return out.reshape(nseq * n_new, ATTN_W)


def _discretize(a_re, a_im, dt):
    th_re = a_re * dt
    th_im = a_im * dt
    mag = jnp.exp(th_re)
    abar_re = mag * jnp.cos(th_im)
    abar_im = mag * jnp.sin(th_im)
    den = a_re * a_re + a_im * a_im
    num_re = abar_re - 1.0
    coef_re = (num_re * a_re + abar_im * a_im) / den
    coef_im = (abar_im * a_re - num_re * a_im) / den
    return th_re, th_im, coef_re, coef_im


def _power(th_re, th_im, n):
    mag = jnp.exp(th_re * n)
    return mag * jnp.cos(th_im * n), mag * jnp.sin(th_im * n)


def _s5_ops_kernel(a_col_ref, a_row_ref, bT_ref, cT_ref, toep_ref, sin_ref, sout_ref, a12_ref, *, t_len):
    p = STATE_DIM
    cg = SSM_GROUP
    tc = t_len * cg
    a_col = a_col_ref[...]
    a_row = a_row_ref[...]
    bT = bT_ref[...]
    cT = cT_ref[...]

    th_re_r, th_im_r, coef_re, coef_im = _discretize(a_row[0:1], a_row[1:2], a_row[2:3])
    is_im = lax.broadcasted_iota(jnp.int32, (1, 2 * p), 1) >= p
    bb_re = coef_re * bT[0] - coef_im * bT[1]
    bb_im = coef_re * bT[1] + coef_im * bT[0]
    bb_a = jnp.where(is_im, bb_im, bb_re)
    bb_b = jnp.where(is_im, bb_re, -bb_im)
    for s in range(t_len):
        pw_re, pw_im = _power(th_re_r, th_im_r, float(t_len - 1 - s))
        sin_ref[s * cg:(s + 1) * cg, :] = (pw_re * bb_a + pw_im * bb_b).astype(sin_ref.dtype)
    at_re, at_im = _power(th_re_r, th_im_r, float(t_len))
    a12_ref[0:1, :] = at_re
    a12_ref[1:2, :] = jnp.where(is_im, at_im, -at_im)

    th_re_c, th_im_c, _, _ = _discretize(a_col[:, 0:1], a_col[:, 1:2], a_col[:, 2:3])
    lane = lax.broadcasted_iota(jnp.int32, (LANES, tc), 1)
    srow = lax.broadcasted_iota(jnp.int32, (LANES, tc), 0)
    tile_c = jnp.where(srow == lane % cg, 1.0, 0.0).astype(F32)
    ccre = _dot_exact(cT[0], tile_c)
    ccim = _dot_exact(cT[1], tile_c)
    t_of_lane = (lax.broadcasted_iota(jnp.int32, (1, tc), 1) // cg).astype(F32)
    pw_re, pw_im = _power(th_re_c, th_im_c, t_of_lane)
    g_re = pw_re * ccre - pw_im * ccim
    g_im = pw_re * ccim + pw_im * ccre
    pw1_re, pw1_im = _power(th_re_c, th_im_c, t_of_lane + 1.0)
    sout_ref[0:p, :] = (pw1_re * ccre - pw1_im * ccim).astype(sout_ref.dtype)
    sout_ref[p:2 * p, :] = (-(pw1_re * ccim + pw1_im * ccre)).astype(sout_ref.dtype)

    r0 = jnp.dot(jnp.where(is_im, -bb_im, bb_re), jnp.concatenate([g_re, g_im], axis=0),
                 precision=lax.Precision.HIGHEST, preferred_element_type=F32)
    lane_r = lax.broadcasted_iota(jnp.int32, (cg, tc), 1)
    for s in range(t_len):
        shifted = r0 if s == 0 else pltpu.roll(r0, s * cg, axis=1)
        toep_ref[s * cg:(s + 1) * cg, :] = jnp.where(lane_r >= s * cg, shifted, 0.0).astype(toep_ref.dtype)


def _s5_operators(a_re, a_im, log_dt, b_re, b_im, c_re, c_im, *, t_len):
    g, p = a_re.shape
    cg = SSM_GROUP
    tc = t_len * cg
    dt = jnp.broadcast_to(jnp.exp(log_dt)[:, None], (g, p))
    a_col = jnp.stack([a_re, a_im, dt], axis=-1)
    a_row = jnp.tile(jnp.stack([a_re, a_im, dt], axis=1), (1, 1, 2))
    bT = jnp.tile(jnp.stack([b_re, b_im], axis=1).transpose(0, 1, 3, 2), (1, 1, 1, 2))
    cT = jnp.stack([c_re, c_im], axis=1).transpose(0, 1, 3, 2)
    cT = jnp.pad(cT, ((0, 0), (0, 0), (0, 0), (0, LANES - cg)))
    per_g = lambda *dims: pl.BlockSpec((None,) + dims, lambda i: (i,) + (0,) * len(dims))
    return pl.pallas_call(
        functools.partial(_s5_ops_kernel, t_len=t_len),
        grid=(g,),
        in_specs=[per_g(p, 3), per_g(3, 2 * p), per_g(2, cg, 2 * p), per_g(2, p, LANES)],
        out_specs=[per_g(tc, tc), per_g(tc, 2 * p), per_g(2 * p, tc), per_g(2, 2 * p)],
        out_shape=[jax.ShapeDtypeStruct((g, tc, tc), BF16),
                   jax.ShapeDtypeStruct((g, tc, 2 * p), BF16),
                   jax.ShapeDtypeStruct((g, 2 * p, tc), BF16),
                   jax.ShapeDtypeStruct((g, 2, 2 * p), F32)],
        compiler_params=pltpu.CompilerParams(dimension_semantics=("arbitrary",)),
        name="s5_operators",
    )(a_col, a_row, bT, cT)


def _s5_chunks_kernel(u_ref, toep_ref, sin_ref, sout_ref, a12_ref, s0_ref, y_ref, sT_ref, *, nb):
    kc = u_ref.shape[1] // nb

    @pl.when(pl.program_id(0) == 0)
    def _():
        sT_ref[...] = s0_ref[...]

    def group(g, _):
        u = u_ref[g]
        s_loc = jnp.dot(u, sin_ref[g], preferred_element_type=F32)
        a12 = a12_ref[g]
        a1, a2 = a12[0:1, :], a12[1:2, :]
        s = sT_ref[g]
        prev = []
        for j in range(kc):
            prev.append(s)
            s = a1 * s + a2 * pltpu.roll(s, STATE_DIM, axis=1) + s_loc[j * nb:(j + 1) * nb, :]
        sT_ref[g] = s
        s_prev = prev[0] if kc == 1 else jnp.concatenate(prev, axis=0)
        y_ref[g] = (jnp.dot(u, toep_ref[g], preferred_element_type=F32)
                    + jnp.dot(s_prev.astype(BF16), sout_ref[g], preferred_element_type=F32))
        return 0

    lax.fori_loop(0, u_ref.shape[0], group, 0)


def _s5_chunks(u3, ops, s0, *, nb, kc):
    toep, s_in, s_out, a12 = ops
    g, rows_total, tc = u3.shape
    rows = kc * nb
    blk = pl.BlockSpec((g, rows, tc), lambda i: (0, i, 0))
    state = pl.BlockSpec((g, nb, 2 * STATE_DIM), lambda i: (0, 0, 0))
    est = (2 * g * rows * tc * (2 + 4) + 2 * sum(o.size * o.dtype.itemsize for o in ops)
           + 4 * g * nb * 2 * STATE_DIM * 4 + 8 * rows * tc * 4)
    return pl.pallas_call(
        functools.partial(_s5_chunks_kernel, nb=nb),
        grid=(rows_total // rows,),
        in_specs=[blk] + [_resident(o.shape) for o in ops] + [state],
        out_specs=[blk, state],
        out_shape=[jax.ShapeDtypeStruct((g, rows_total, tc), F32),
                   jax.ShapeDtypeStruct((g, nb, 2 * STATE_DIM), F32)],
        compiler_params=pltpu.CompilerParams(dimension_semantics=("arbitrary",),
                                             vmem_limit_bytes=_vmem_limit(est)),
        name="s5_chunks",
    )(u3, toep, s_in, s_out, a12, s0)


def _s5(u, ssm, h0_re, h0_im, *, batch, t_len, kc):
    n, w = u.shape
    g = w // SSM_GROUP
    seq = n // batch
    nk = seq // t_len
    ops = _s5_operators(*ssm, t_len=t_len)
    u3 = (u.astype(BF16).reshape(batch, nk, t_len, g, SSM_GROUP)
          .transpose(3, 1, 0, 2, 4).reshape(g, nk * batch, t_len * SSM_GROUP))
    if h0_re is None:
        s0 = jnp.zeros((g, batch, 2 * STATE_DIM), F32)
    else:
        s0 = jnp.concatenate([h0_re, h0_im], axis=-1).transpose(1, 0, 2)
    y3, sT = _s5_chunks(u3, ops, s0, nb=batch, kc=kc)
    y = (y3.reshape(g, nk, batch, t_len, SSM_GROUP).transpose(2, 1, 3, 0, 4).reshape(n, w))
    sT = sT.transpose(1, 0, 2)
    return y, sT[..., :STATE_DIM], sT[..., STATE_DIM:]


def _mix_ffn2_kernel(x1_ref, attn_ref, yc_ref, u_ref, gm_ref, wgate_ref, wao_ref, dskip_ref, wga_ref, wgb_ref,
                     wo_ref, g2_ref, wg_ref, wu_ref, wd_ref, gf_ref, y_ref, *, last_layer):
    x1 = x1_ref[...]
    d = x1.shape[1]
    h = _rmsnorm(x1, gm_ref[...]).astype(BF16)
    gates = jax.nn.sigmoid(jnp.dot(h, wgate_ref[...], preferred_element_type=F32))
    y_attn = jnp.dot(attn_ref[...], wao_ref[...], preferred_element_type=F32)
    z = jax.nn.gelu(yc_ref[...] + dskip_ref[...] * u_ref[...]).astype(BF16)
    y_glu = (jnp.dot(z, wga_ref[...], preferred_element_type=F32)
             * jax.nn.sigmoid(jnp.dot(z, wgb_ref[...], preferred_element_type=F32)))
    mixed = (gates[:, :d] * y_attn + gates[:, d:] * y_glu).astype(BF16)
    x2 = x1 + jnp.dot(mixed, wo_ref[...], preferred_element_type=F32)
    x3 = _swiglu_half(x2, g2_ref, wg_ref, wu_ref, wd_ref)
    y_ref[...] = _rmsnorm(x3, gf_ref[...]) if last_layer else x3


def _mix_ffn2(x1, attn, yc, u, gm, wgate, wao, dskip, wga, wgb, wo, g2, wg, wu, wd, gf, *, tm, last_layer):
    n, d = x1.shape
    fc = wg.shape[2]
    tok = lambda w: pl.BlockSpec((tm, w), lambda i: (i, 0))
    weights = (gm, wgate, wao, dskip, wga, wgb, wo, g2, wg, wu, wd, gf)
    est = (sum(w.size * w.dtype.itemsize for w in weights)
           + 2 * tm * (2 * d * 4 + attn.shape[1] * 2 + 2 * u.shape[1] * 4)
           + tm * (8 * d * 4 + 2 * fc * 4))
    return pl.pallas_call(
        functools.partial(_mix_ffn2_kernel, last_layer=last_layer),
        grid=(n // tm,),
        in_specs=[tok(d), tok(attn.shape[1]), tok(yc.shape[1]), tok(u.shape[1])]
                 + [_resident(w.shape) for w in weights],
        out_specs=tok(d),
        out_shape=jax.ShapeDtypeStruct((n, d), F32),
        compiler_params=pltpu.CompilerParams(dimension_semantics=("arbitrary",),
                                             vmem_limit_bytes=_vmem_limit(est)),
        name="mix_ffn2",
    )(x1, attn, yc, u, *weights)


FFN_CHUNK = 256
TOKEN_BLOCK = 256
ATTN_BLOCK = 256
S5_CHUNK = 16
S5_CHUNKS_PER_STEP = 8


def _chunked_ffn_weights(w_gate, w_up, w_down):
    d, f = w_gate.shape
    nc = f // FFN_CHUNK
    wg = w_gate.astype(BF16).reshape(d, nc, FFN_CHUNK).transpose(1, 0, 2)
    wu = w_up.astype(BF16).reshape(d, nc, FFN_CHUNK).transpose(1, 0, 2)
    wd = w_down.astype(BF16).reshape(nc, FFN_CHUNK, d)
    return wg, wu, wd


def kernel(x_prompt, x_sample, cache_k, cache_v, cache_logf, state_ssm_re, state_ssm_im, page_table, norm_ffn1, w_ffn1_gate, w_ffn1_up, w_ffn1_down, norm_mix, w_in, b_forget, ssm_a_re, ssm_a_im, ssm_log_dt, ssm_b_re, ssm_b_im, ssm_c_re, ssm_c_im, ssm_d, w_attn_out, w_glu_a, w_glu_b, w_o, norm_ffn2, w_ffn2_gate, w_ffn2_up, w_ffn2_down, norm_final):
    depth = norm_ffn1.shape[0]
    batch, seq, d = x_prompt.shape
    nseq, n_new, _ = x_sample.shape
    ssm_w = ssm_d.shape[1] * ssm_d.shape[2]
    f0 = 3 * ATTN_W
    u0 = f0 + N_HEADS
    ga0 = u0 + ssm_w
    page = cache_k.shape[2]

    xp = x_prompt.reshape(batch * seq, d)
    xs = x_sample.reshape(nseq * n_new, d)
    outs = {name: [] for name in ("kp", "vp", "fp", "ks", "vs", "fs", "hrp", "hip", "hrs", "his")}
    row = lambda a: a.reshape(1, -1)
    for l in range(depth):
        ffn1 = _chunked_ffn_weights(w_ffn1_gate[l], w_ffn1_up[l], w_ffn1_down[l])
        ffn2 = _chunked_ffn_weights(w_ffn2_gate[l], w_ffn2_up[l], w_ffn2_down[l])
        w_in_b = w_in[l].astype(BF16)
        wfT = jnp.pad(w_in_b[:, f0:u0].T, ((0, 16 - N_HEADS), (0, 0)))
        in_w = (row(norm_ffn1[l]), *ffn1, row(norm_mix[l]), w_in_b[:, :f0], wfT,
                b_forget[l].reshape(N_HEADS, 1), w_in_b[:, u0:ga0])
        mix_w = (row(norm_mix[l]), w_in_b[:, ga0:], w_attn_out[l].astype(BF16), row(ssm_d[l]),
                 w_glu_a[l].astype(BF16), w_glu_b[l].astype(BF16), w_o[l].astype(BF16),
                 row(norm_ffn2[l]), *ffn2, row(norm_final))
        ssm = (ssm_a_re[l], ssm_a_im[l], ssm_log_dt[l], ssm_b_re[l], ssm_b_im[l], ssm_c_re[l], ssm_c_im[l])

        x1, q, kb, vb, kf, vf, lfT, u = _ffn1_inproj(xp, *in_w, tm=TOKEN_BLOCK)
        attn = _attn_prompt(q, kb, vb, lfT, batch=batch, tq=ATTN_BLOCK)
        yc, hr, hi = _s5(u, ssm, None, None, batch=batch, t_len=S5_CHUNK, kc=S5_CHUNKS_PER_STEP)
        xp = _mix_ffn2(x1, attn, yc, u, *mix_w, tm=TOKEN_BLOCK, last_layer=l == depth - 1)
        outs["kp"].append(kf.reshape(batch, seq, N_HEADS, HEAD_DIM))
        outs["vp"].append(vf.reshape(batch, seq, N_HEADS, HEAD_DIM))
        outs["fp"].append(lfT.T.reshape(batch, seq, N_HEADS))
        outs["hrp"].append(hr)
        outs["hip"].append(hi)

        x1, q, kb, vb, kf, vf, lfT, u = _ffn1_inproj(xs, *in_w, tm=TOKEN_BLOCK)
        lf_new = lfT.reshape(N_HEADS, nseq, n_new).transpose(1, 0, 2)
        lf_new = jnp.pad(lf_new, ((0, 0), (0, 0), (0, LANES - n_new)))
        n_phys = cache_k.shape[1]
        attn = _attn_sample(page_table, q, kf, vf, lf_new,
                            cache_k[l].reshape(n_phys, page, ATTN_W),
                            cache_v[l].reshape(n_phys, page, ATTN_W),
                            cache_logf[l].transpose(0, 2, 1), n_new=n_new)
        yc, hr, hi = _s5(u, ssm, state_ssm_re[l], state_ssm_im[l], batch=nseq, t_len=n_new, kc=1)
        xs = _mix_ffn2(x1, attn, yc, u, *mix_w, tm=TOKEN_BLOCK, last_layer=l == depth - 1)
        outs["ks"].append(kf.reshape(nseq, n_new, N_HEADS, HEAD_DIM))
        outs["vs"].append(vf.reshape(nseq, n_new, N_HEADS, HEAD_DIM))
        outs["fs"].append(lfT.T.reshape(nseq, n_new, N_HEADS))
        outs["hrs"].append(hr)
        outs["his"].append(hi)

    stack = lambda name: jnp.stack(outs[name])
    return (xp.reshape(batch, seq, d), xs.reshape(nseq, n_new, d),
            stack("kp"), stack("vp"), stack("fp"), stack("ks"), stack("vs"), stack("fs"),
            stack("hrp"), stack("hip"), stack("hrs"), stack("his"))
```

```python
import functools
import math

import jax
import jax.numpy as jnp
from jax import lax
from jax.experimental import pallas as pl
from jax.experimental.pallas import tpu as pltpu

F32 = jnp.float32
BF16 = jnp.bfloat16

N_HEADS = 8
HEAD_DIM = 64
ATTN_W = N_HEADS * HEAD_DIM
SSM_GROUP = 16
STATE_DIM = 64
EPS = 1e-6
LOG2E = math.log2(math.e)
LANES = 128
SUBLANES = 8
MXU_DIM = 256
V7X_VMEM_BYTES = 64 * 1024 * 1024
VMEM_CAP_BYTES = V7X_VMEM_BYTES - 8 * 1024 * 1024
NEG_BIG = -1e30
PAGES_PER_STEP = 8

FFN_CHUNK = MXU_DIM
TOKEN_BLOCK = 256
ATTN_BLOCK = 256
SCAN_LANES = 512


def _vmem_limit(estimate_bytes):
    return int(min(VMEM_CAP_BYTES, max(32 * 1024 * 1024, estimate_bytes * 5 // 4)))


def _resident(shape):
    nd = len(shape)
    return pl.BlockSpec(shape, lambda *_: (0,) * nd, pipeline_mode=pl.Buffered(1))


def _nbytes(arrays):
    return sum(a.size * a.dtype.itemsize for a in arrays)


def _rmsnorm(x, g):
    return x * lax.rsqrt(jnp.mean(x * x, axis=-1, keepdims=True) + EPS) * g


def _swiglu_half(x, g_ref, wg_ref, wu_ref, wd_ref):
    h = _rmsnorm(x, g_ref[...]).astype(BF16)
    acc = jnp.zeros(x.shape, F32)
    for c in range(wg_ref.shape[0]):
        gate = jnp.dot(h, wg_ref[c], preferred_element_type=F32)
        up = jnp.dot(h, wu_ref[c], preferred_element_type=F32)
        a = (gate * jax.nn.sigmoid(gate) * up).astype(BF16)
        acc = acc + jnp.dot(a, wd_ref[c], preferred_element_type=F32)
    return x + 0.5 * acc


def _log_sigmoid(x):
    return jnp.minimum(x, 0.0) - jnp.log1p(jnp.exp(-jnp.abs(x)))


def _split3(x):
    rnd = lambda a: a.astype(BF16).astype(F32)
    hi = rnd(x)
    mid = rnd(x - hi)
    lo = rnd(x - hi - mid)
    return hi, mid, lo


def _s5_discretize_kernel(a_re_ref, a_im_ref, dt_ref, bt_re_ref, bt_im_ref,
                          abar_re_ref, abar_im_ref, bbar_re_ref, bbar_im_ref):
    a_re, a_im, dt = a_re_ref[...], a_im_ref[...], dt_ref[...]
    mag = jnp.exp(a_re * dt)
    abar_re = mag * jnp.cos(a_im * dt)
    abar_im = mag * jnp.sin(a_im * dt)
    den = a_re * a_re + a_im * a_im
    num_re = abar_re - 1.0
    coef_re = (num_re * a_re + abar_im * a_im) / den
    coef_im = (abar_im * a_re - num_re * a_im) / den
    abar_re_ref[...] = abar_re
    abar_im_ref[...] = abar_im
    bt_re, bt_im = bt_re_ref[...], bt_im_ref[...]
    bbar_re_ref[...] = coef_re * bt_re - coef_im * bt_im
    bbar_im_ref[...] = coef_re * bt_im + coef_im * bt_re


def _s5_weights(a_re, a_im, log_dt, b_re, b_im, c_re, c_im):
    g, p = a_re.shape
    c = b_re.shape[2]
    dt = jnp.broadcast_to(jnp.exp(log_dt)[:, None, None], (g, 1, p))
    gp = jax.ShapeDtypeStruct((g, 1, p), F32)
    gcp = jax.ShapeDtypeStruct((g, c, p), F32)
    abar_re, abar_im, bbar_re, bbar_im = pl.pallas_call(
        _s5_discretize_kernel, out_shape=[gp, gp, gcp, gcp], name="s5_discretize",
    )(a_re[:, None, :], a_im[:, None, :], dt, b_re.transpose(0, 2, 1), b_im.transpose(0, 2, 1))
    eye = jnp.eye(g, dtype=F32)

    def in_tiles(bbar_t):
        full = jnp.einsum('gcp,gh->gchp', bbar_t, eye).reshape(g * c, g * p)
        n_tiles = g * p // MXU_DIM
        per_lane_tile = LANES * n_tiles // (g * c)
        return jnp.stack([full[LANES * (n // per_lane_tile):LANES * (n // per_lane_tile + 1),
                               MXU_DIM * n:MXU_DIM * (n + 1)] for n in range(n_tiles)]).astype(BF16)

    def out_tiles(c_mat):
        full = jnp.einsum('gcp,gh->gphc', c_mat, eye).reshape(g * p, g * c)
        n_tiles = g * c // LANES
        k = g * p // n_tiles
        return jnp.stack([full[k * j:k * (j + 1), LANES * j:LANES * (j + 1)]
                          for j in range(n_tiles)]).astype(BF16)

    return (in_tiles(bbar_re), in_tiles(bbar_im), out_tiles(c_re), out_tiles(-c_im),
            abar_re.reshape(1, g * p), abar_im.reshape(1, g * p))


def _ffn1_mixer_in_kernel(x_ref, h0re_ref, h0im_ref, g1_ref, wg_ref, wu_ref, wd_ref, g2_ref, wqkv_ref, wf_ref,
                          bf_ref, wus_ref, bre_ref, bim_ref, cre_ref, cimn_ref, are_ref, aim_ref, dskip_ref,
                          x1_ref, q_ref, k_ref, v_ref, lf_ref, y_ref, hTre_ref, hTim_ref,
                          sre_ref, sim_ref):
    nb, tt, d = x_ref.shape
    tm = nb * tt
    x1 = _swiglu_half(x_ref[...].reshape(tm, d), g1_ref, wg_ref, wu_ref, wd_ref)
    x1_ref[...] = x1.reshape(nb, tt, d)
    h = _rmsnorm(x1, g2_ref[...]).astype(BF16)
    zqkv = jnp.dot(h, wqkv_ref[...], preferred_element_type=F32)
    q_ref[...] = (zqkv[:, :ATTN_W] * (LOG2E * HEAD_DIM ** -0.5)).astype(q_ref.dtype).reshape(nb, tt, ATTN_W)
    k_ref[...] = zqkv[:, ATTN_W:2 * ATTN_W].reshape(nb, tt, ATTN_W)
    v_ref[...] = zqkv[:, 2 * ATTN_W:].reshape(nb, tt, ATTN_W)
    zf = jnp.dot(h, wf_ref[...], preferred_element_type=F32)
    lf_ref[...] = _log_sigmoid(zf + bf_ref[...]).reshape(nb, tt, LANES)

    u = jnp.dot(h, wus_ref[...], preferred_element_type=F32)
    i0 = lax.broadcasted_iota(jnp.int32, (tm, tm), 0)
    i1 = lax.broadcasted_iota(jnp.int32, (tm, tm), 1)
    to_step_major = jnp.where(i1 == (i0 % nb) * tt + i0 // nb, 1.0, 0.0).astype(BF16)
    to_seq_major = jnp.where(i0 == (i1 % nb) * tt + i1 // nb, 1.0, 0.0).astype(BF16)
    ub = jnp.dot(to_step_major, u.astype(BF16), preferred_element_type=F32).astype(BF16)
    n_in = bre_ref.shape[0]
    lane_tiles = ub.shape[1] // LANES
    for n in range(n_in):
        j = n * lane_tiles // n_in
        ut = ub[:, j * LANES:(j + 1) * LANES]
        sre_ref[:, n * MXU_DIM:(n + 1) * MXU_DIM] = jnp.dot(ut, bre_ref[n], preferred_element_type=F32)
        sim_ref[:, n * MXU_DIM:(n + 1) * MXU_DIM] = jnp.dot(ut, bim_ref[n], preferred_element_type=F32)

    @pl.when(pl.program_id(1) == 0)
    def _():
        hTre_ref[...] = h0re_ref[...]
        hTim_ref[...] = h0im_ref[...]

    n_state = sre_ref.shape[1]
    for c0 in range(0, n_state, SCAN_LANES):
        cs = slice(c0, c0 + SCAN_LANES)
        ar = jnp.broadcast_to(are_ref[:, cs], (SUBLANES, SCAN_LANES))
        ai = jnp.broadcast_to(aim_ref[:, cs], (SUBLANES, SCAN_LANES))
        for a in range(nb // SUBLANES):
            seqs = slice(a * SUBLANES, (a + 1) * SUBLANES)

            def step(t, carry, a=a, cs=cs, ar=ar, ai=ai):
                sr, si = carry
                rows = pl.ds(pl.multiple_of(t * nb + a * SUBLANES, SUBLANES), SUBLANES)
                nr = ar * sr - ai * si + sre_ref[rows, cs]
                ni = ar * si + ai * sr + sim_ref[rows, cs]
                sre_ref[rows, cs] = nr
                sim_ref[rows, cs] = ni
                return nr, ni

            sr, si = lax.fori_loop(0, tt, step, (hTre_ref[seqs, cs], hTim_ref[seqs, cs]))
            hTre_ref[seqs, cs] = sr
            hTim_ref[seqs, cs] = si

    n_out = cre_ref.shape[0]
    kw = n_state // n_out
    ys = []
    for j in range(n_out):
        hr = sre_ref[:, j * kw:(j + 1) * kw].astype(BF16)
        hi = sim_ref[:, j * kw:(j + 1) * kw].astype(BF16)
        ys.append(jnp.dot(hr, cre_ref[j], preferred_element_type=F32)
                  + jnp.dot(hi, cimn_ref[j], preferred_element_type=F32))
    y = sum(jnp.dot(to_seq_major, piece.astype(BF16), preferred_element_type=F32)
            for piece in _split3(jnp.concatenate(ys, axis=1)))
    y_ref[...] = (y + dskip_ref[...] * u).reshape(nb, tt, y.shape[1])


def _ffn1_mixer_in(x, h0_re, h0_im, weights, *, nb, tt):
    batch, seq, d = x.shape
    n_state = h0_re.shape[1]
    ssm_w = weights[8].shape[1]
    tm = nb * tt
    tok = lambda w: pl.BlockSpec((nb, tt, w), lambda i, j: (i, j, 0))
    state = pl.BlockSpec((nb, n_state), lambda i, j: (i, 0))
    widths = (d, ATTN_W, ATTN_W, ATTN_W, LANES, ssm_w)
    dtypes = (F32, BF16, F32, F32, F32, F32)
    est = (_nbytes(weights) + 2 * tm * sum(w * jnp.dtype(t).itemsize for w, t in zip(widths, dtypes))
           + 2 * tm * d * 4 + 2 * tm * n_state * 4 + 8 * nb * n_state * 4
           + tm * (3 * d * 4 + 2 * FFN_CHUNK * 4 + 3 * ATTN_W * 4 + 2 * ssm_w * 4))
    return pl.pallas_call(
        _ffn1_mixer_in_kernel,
        grid=(batch // nb, seq // tt),
        in_specs=[tok(d), state, state] + [_resident(w.shape) for w in weights],
        out_specs=[tok(w) for w in widths] + [state, state],
        out_shape=[jax.ShapeDtypeStruct((batch, seq, w), t) for w, t in zip(widths, dtypes)]
                  + [jax.ShapeDtypeStruct((batch, n_state), F32)] * 2,
        scratch_shapes=[pltpu.VMEM((tm, n_state), F32)] * 2,
        compiler_params=pltpu.CompilerParams(dimension_semantics=("arbitrary", "arbitrary"),
                                             vmem_limit_bytes=_vmem_limit(est)),
        name="ffn1_mixer_in",
    )(x, h0_re, h0_im, *weights)


def _attn_prompt_kernel(q_ref, k_ref, v_ref, lf_ref, o_ref, c_ref, kaug_ref, vt_ref, *, tq):
    seq = q_ref.shape[0]
    nq = seq // tq
    r = lax.broadcasted_iota(jnp.int32, (LANES, LANES), 0)
    cc = lax.broadcasted_iota(jnp.int32, (LANES, LANES), 1)
    tri = jnp.where(cc <= r, 1.0, 0.0).astype(F32)
    carry = jnp.zeros((1, LANES), F32)
    for blk in range(seq // LANES):
        pieces = _split3(lf_ref[blk * LANES:(blk + 1) * LANES, :])
        cb = sum(jnp.dot(tri, x, preferred_element_type=F32) for x in pieces) + carry
        carry = cb[LANES - 1:LANES, :]
        c_ref[blk * LANES:(blk + 1) * LANES, :] = cb * LOG2E
    c_pieces = jnp.concatenate(_split3(c_ref[...]), axis=1)

    lane = lax.broadcasted_iota(jnp.int32, (tq, LANES), 1)
    first = lane < HEAD_DIM
    key_pos = lax.broadcasted_iota(jnp.int32, (tq, tq), 0)
    qry_pos = lax.broadcasted_iota(jnp.int32, (tq, tq), 1)
    src = lax.broadcasted_iota(jnp.int32, (3 * LANES, LANES), 0)
    dst = lax.broadcasted_iota(jnp.int32, (3 * LANES, LANES), 1)

    pairs = range(N_HEADS // 2)
    cols = [slice(hp * LANES, (hp + 1) * LANES) for hp in pairs]
    for hp in pairs:
        sel = jnp.where((dst < 6) & (src == (dst % 3) * LANES + 2 * hp + dst // 3), 1.0, 0.0).astype(F32)
        kaug_ref[hp, :, :LANES] = k_ref[:, cols[hp]].astype(BF16)
        kaug_ref[hp, :, LANES:] = jnp.dot(c_pieces, sel, preferred_element_type=F32).astype(BF16)
        for j in range(nq):
            vt_ref[hp, j] = v_ref[j * tq:(j + 1) * tq, cols[hp]].T.astype(BF16)

    def q_block(qi, _):
        r0 = pl.multiple_of(qi * tq, tq)
        qaug = []
        for hp in pairs:
            q = q_ref[pl.ds(r0, tq), cols[hp]]
            for e in range(2):
                qe = jnp.where(first, q, 0) if e == 0 else jnp.where(first, 0, q)
                minus_one = jnp.where((lane >= 3 * e) & (lane < 3 * e + 3), -1.0, 0.0).astype(BF16)
                qaug.append(jnp.concatenate([qe, minus_one], axis=1))

        def step(c0, j, states, mask):
            scores = []
            for hp in pairs:
                kb = kaug_ref[hp, pl.ds(c0, tq), :]
                for e in range(2):
                    s = lax.dot_general(kb, qaug[2 * hp + e], (((1,), (1,)), ((), ())),
                                        preferred_element_type=F32)
                    scores.append(s if mask is None else jnp.where(mask, s, NEG_BIG))
            probs, stats = [], []
            for h, s in enumerate(scores):
                m, l, _ = states[h]
                m_new = jnp.maximum(m, jnp.max(s, axis=0, keepdims=True))
                alpha = jnp.exp2(m - m_new)
                p = jnp.exp2(s - m_new)
                stats.append((m_new, alpha, alpha * l + jnp.sum(p, axis=0, keepdims=True)))
                probs.append(p.astype(BF16))
            new = []
            for h, p in enumerate(probs):
                vth = vt_ref[h // 2, j][(h % 2) * HEAD_DIM:(h % 2 + 1) * HEAD_DIM, :]
                m_new, alpha, l = stats[h]
                new.append((m_new, l, alpha * states[h][2] + jnp.dot(vth, p, preferred_element_type=F32)))
            return tuple(new)

        def kv_step(j, states):
            return step(pl.multiple_of(j * tq, tq), j, states, None)

        init = (jnp.full((1, tq), NEG_BIG, F32), jnp.zeros((1, tq), F32), jnp.zeros((HEAD_DIM, tq), F32))
        states = lax.fori_loop(0, qi, kv_step, (init,) * N_HEADS)
        states = step(r0, qi, states, key_pos <= qry_pos)
        for hp in pairs:
            ot = jnp.concatenate([acc / l for _, l, acc in states[2 * hp:2 * hp + 2]], axis=0)
            o_ref[pl.ds(r0, tq), cols[hp]] = ot.T.astype(o_ref.dtype)
        return 0

    lax.fori_loop(0, nq, q_block, 0)


def _attn_prompt(q, k, v, lf, *, tq):
    batch, seq, _ = q.shape
    blk = lambda w: pl.BlockSpec((None, seq, w), lambda b: (b, 0, 0))
    est = (2 * seq * ATTN_W * (2 + 4 + 4 + 2) + 2 * seq * LANES * 4 + seq * LANES * 4 * 5
           + seq * 2 * LANES * 2 * 2 + 24 * tq * tq * 4)
    return pl.pallas_call(
        functools.partial(_attn_prompt_kernel, tq=tq),
        grid=(batch,),
        in_specs=[blk(ATTN_W), blk(ATTN_W), blk(ATTN_W), blk(LANES)],
        out_specs=blk(ATTN_W),
        out_shape=jax.ShapeDtypeStruct((batch, seq, ATTN_W), BF16),
        scratch_shapes=[pltpu.VMEM((seq, LANES), F32),
                        pltpu.VMEM((N_HEADS // 2, seq, 2 * LANES), BF16),
                        pltpu.VMEM((N_HEADS // 2, seq // tq, LANES, tq), BF16)],
        compiler_params=pltpu.CompilerParams(dimension_semantics=("arbitrary",),
                                             vmem_limit_bytes=_vmem_limit(est)),
        name="attn_prompt",
    )(q, k, v, lf)


def _suffix_and_total(x):
    r = lax.broadcasted_iota(jnp.int32, (LANES, 2 * LANES), 0)
    c = lax.broadcasted_iota(jnp.int32, (LANES, 2 * LANES), 1)
    w = jnp.where((c >= LANES) | (r > c), 1.0, 0.0).astype(F32)
    both = sum(jnp.dot(p, w, preferred_element_type=F32) for p in _split3(x))
    return both[:, :LANES], both[:, LANES:]


def _attn_sample_kernel(pt_ref, q_ref, kn_ref, vn_ref, lfn_ref, *refs, n_new):
    del pt_ref
    pp = PAGES_PER_STEP
    k_refs, v_refs, lf_refs = refs[:pp], refs[pp:2 * pp], refs[2 * pp:3 * pp]
    o_ref = refs[3 * pp]
    qbd_ref, m_ref, l_ref, acc_ref, carry_ref, r_ref = refs[3 * pp + 1:]
    rows = N_HEADS * n_new
    step = pl.program_id(1)
    hd_row = lax.broadcasted_iota(jnp.int32, (rows, ATTN_W), 0) // n_new
    hd_col = lax.broadcasted_iota(jnp.int32, (rows, ATTN_W), 1) // HEAD_DIM

    @pl.when(step == 0)
    def _():
        qt = jnp.concatenate([q_ref[...]] * N_HEADS, axis=0)
        qbd_ref[...] = jnp.where(hd_row == hd_col, qt, 0.0).astype(BF16)
        m_ref[...] = jnp.full(m_ref.shape, NEG_BIG, F32)
        l_ref[...] = jnp.zeros(l_ref.shape, F32)
        acc_ref[...] = jnp.zeros(acc_ref.shape, F32)
        carry_ref[...] = jnp.zeros(carry_ref.shape, F32)

    qbd = qbd_ref[...]

    def add_head_bias(s, slot):
        return jnp.concatenate(
            [s[h * n_new:(h + 1) * n_new, :] + r_ref[slot, h:h + 1, :] for h in range(N_HEADS)], axis=0)

    def online_update(s_list, pv_list):
        m_old = m_ref[...]
        m_new = m_old
        for s in s_list:
            m_new = jnp.maximum(m_new, jnp.max(s, axis=1, keepdims=True))
        alpha = jnp.exp2(m_old - m_new)
        l = alpha * l_ref[...]
        acc = alpha * acc_ref[...]
        for s, pv in zip(s_list, pv_list):
            p = jnp.exp2(s - m_new)
            l = l + jnp.sum(p, axis=1, keepdims=True)
            acc = acc + pv(p.astype(BF16))
        m_ref[...] = m_new
        l_ref[...] = l
        acc_ref[...] = acc

    carry = carry_ref[...]
    for i in reversed(range(pp)):
        suffix, total = _suffix_and_total(lf_refs[i][...])
        r_ref[i] = (suffix + carry) * LOG2E
        carry = carry + total
    carry_ref[...] = carry
    nt = (((1,), (1,)), ((), ()))
    online_update(
        [add_head_bias(jnp.dot(qbd, k_refs[i][...].astype(BF16), preferred_element_type=F32), i)
         for i in range(pp)],
        [lambda p, i=i: lax.dot_general(p, v_refs[i][...].astype(BF16), nt, preferred_element_type=F32)
         for i in range(pp)])

    @pl.when(step == pl.num_programs(1) - 1)
    def _():
        suffix, total = _suffix_and_total(lfn_ref[...])
        r_ref[0] = (suffix - total) * LOG2E
        pad = jnp.zeros((LANES - n_new, ATTN_W), F32)
        kn = jnp.concatenate([kn_ref[...], pad], axis=0).astype(BF16)
        vn = jnp.concatenate([vn_ref[...], pad], axis=0).astype(BF16)
        s = add_head_bias(lax.dot_general(qbd, kn, nt, preferred_element_type=F32), 0)
        qi = lax.broadcasted_iota(jnp.int32, (rows, LANES), 0) % n_new
        kj = lax.broadcasted_iota(jnp.int32, (rows, LANES), 1)
        s = jnp.where(kj <= qi, s, NEG_BIG)
        online_update([s], [lambda p: jnp.dot(p, vn, preferred_element_type=F32)])
        o = acc_ref[...] / l_ref[...]
        o = jnp.where(hd_row == hd_col, o, 0.0)
        out = o[:n_new]
        for h in range(1, N_HEADS):
            out = out + o[h * n_new:(h + 1) * n_new]
        o_ref[...] = out.astype(o_ref.dtype)


def _attn_sample(page_table, q, k_new, v_new, lf_new_t, cache_kt, cache_vt, cache_lft):
    nseq, n_pages = page_table.shape
    n_new = q.shape[1]
    pp = PAGES_PER_STEP
    rows = N_HEADS * n_new
    page = cache_kt.shape[2]

    def paged(r, i):
        def index_map(b, s, pt):
            return (pt[b * n_pages + n_pages - pp * (s + 1) + i], 0, 0)
        return pl.BlockSpec((None, r, page), index_map)

    per_seq = lambda r, w: pl.BlockSpec((None, r, w), lambda b, s, pt: (b, 0, 0))
    in_specs = ([per_seq(n_new, ATTN_W)] * 3 + [per_seq(N_HEADS, LANES)]
                + [paged(ATTN_W, i) for i in range(pp)]
                + [paged(ATTN_W, i) for i in range(pp)]
                + [paged(N_HEADS, i) for i in range(pp)])
    est = 2 * 2 * pp * page * ATTN_W * 4 + 8 * rows * ATTN_W * 4 + 4 * pp * rows * page * 4
    return pl.pallas_call(
        functools.partial(_attn_sample_kernel, n_new=n_new),
        grid_spec=pltpu.PrefetchScalarGridSpec(
            num_scalar_prefetch=1,
            grid=(nseq, n_pages // pp),
            in_specs=in_specs,
            out_specs=per_seq(n_new, ATTN_W),
            scratch_shapes=[pltpu.VMEM((rows, ATTN_W), BF16),
                            pltpu.VMEM((rows, 1), F32),
                            pltpu.VMEM((rows, 1), F32),
                            pltpu.VMEM((rows, ATTN_W), F32),
                            pltpu.VMEM((N_HEADS, LANES), F32),
                            pltpu.VMEM((pp, N_HEADS, LANES), F32)]),
        out_shape=jax.ShapeDtypeStruct((nseq, n_new, ATTN_W), BF16),
        compiler_params=pltpu.CompilerParams(dimension_semantics=("arbitrary", "arbitrary"),
                                             vmem_limit_bytes=_vmem_limit(est)),
        name="attn_sample",
    )(page_table.reshape(-1), q.astype(F32), k_new, v_new, lf_new_t,
      *([cache_kt] * pp), *([cache_vt] * pp), *([cache_lft] * pp))


def _mix_ffn2_kernel(x1_ref, attn_ref, ys_ref, gm_ref, wgate_ref, wao_ref, wga_ref, wgb_ref,
                     wo_ref, g2_ref, wg_ref, wu_ref, wd_ref, gf_ref, y_ref, *, last_layer):
    x1 = x1_ref[...]
    d = x1.shape[1]
    h = _rmsnorm(x1, gm_ref[...]).astype(BF16)
    y_attn = jnp.dot(attn_ref[...], wao_ref[...], preferred_element_type=F32)
    mixed = jax.nn.sigmoid(jnp.dot(h, wgate_ref[:, :d], preferred_element_type=F32)) * y_attn
    z = jax.nn.gelu(ys_ref[...]).astype(BF16)
    y_glu = (jnp.dot(z, wga_ref[...], preferred_element_type=F32)
             * jax.nn.sigmoid(jnp.dot(z, wgb_ref[...], preferred_element_type=F32)))
    mixed = mixed + jax.nn.sigmoid(jnp.dot(h, wgate_ref[:, d:], preferred_element_type=F32)) * y_glu
    x2 = x1 + jnp.dot(mixed.astype(BF16), wo_ref[...], preferred_element_type=F32)
    x3 = _swiglu_half(x2, g2_ref, wg_ref, wu_ref, wd_ref)
    y_ref[...] = _rmsnorm(x3, gf_ref[...]) if last_layer else x3


def _mix_ffn2(x1, attn, ys, weights, *, tm, last_layer):
    n, d = x1.shape
    tok = lambda w: pl.BlockSpec((tm, w), lambda i: (i, 0))
    est = (_nbytes(weights) + 2 * tm * (2 * d * 4 + attn.shape[1] * 2 + ys.shape[1] * 4)
           + tm * (7 * d * 4 + 2 * FFN_CHUNK * 4))
    return pl.pallas_call(
        functools.partial(_mix_ffn2_kernel, last_layer=last_layer),
        grid=(n // tm,),
        in_specs=[tok(d), tok(attn.shape[1]), tok(ys.shape[1])] + [_resident(w.shape) for w in weights],
        out_specs=tok(d),
        out_shape=jax.ShapeDtypeStruct((n, d), F32),
        compiler_params=pltpu.CompilerParams(dimension_semantics=("arbitrary",),
                                             vmem_limit_bytes=_vmem_limit(est)),
        name="mix_ffn2",
    )(x1, attn, ys, *weights)


def _chunked_ffn_weights(w_gate, w_up, w_down):
    d, f = w_gate.shape
    nc = f // FFN_CHUNK
    wg = w_gate.astype(BF16).reshape(d, nc, FFN_CHUNK).transpose(1, 0, 2)
    wu = w_up.astype(BF16).reshape(d, nc, FFN_CHUNK).transpose(1, 0, 2)
    wd = w_down.astype(BF16).reshape(nc, FFN_CHUNK, d)
    return wg, wu, wd


def kernel(x_prompt, x_sample, cache_k, cache_v, cache_logf, state_ssm_re, state_ssm_im, page_table, norm_ffn1, w_ffn1_gate, w_ffn1_up, w_ffn1_down, norm_mix, w_in, b_forget, ssm_a_re, ssm_a_im, ssm_log_dt, ssm_b_re, ssm_b_im, ssm_c_re, ssm_c_im, ssm_d, w_attn_out, w_glu_a, w_glu_b, w_o, norm_ffn2, w_ffn2_gate, w_ffn2_up, w_ffn2_down, norm_final):
    depth = norm_ffn1.shape[0]
    batch, seq, d = x_prompt.shape
    nseq, n_new, _ = x_sample.shape
    n_groups, group_w = ssm_d.shape[1:]
    ssm_w = n_groups * group_w
    n_state = n_groups * ssm_a_re.shape[2]
    f0 = 3 * ATTN_W
    u0 = f0 + N_HEADS
    ga0 = u0 + ssm_w
    n_phys, page = cache_k.shape[1:3]

    xp, xs = x_prompt, x_sample
    outs = {name: [] for name in ("kp", "vp", "fp", "ks", "vs", "fs", "hrp", "hip", "hrs", "his")}
    row = lambda a: a.reshape(1, -1)
    heads = lambda a: a.reshape(a.shape[0], a.shape[1], N_HEADS, HEAD_DIM)
    for l in range(depth):
        last = l == depth - 1
        ffn1 = _chunked_ffn_weights(w_ffn1_gate[l], w_ffn1_up[l], w_ffn1_down[l])
        ffn2 = _chunked_ffn_weights(w_ffn2_gate[l], w_ffn2_up[l], w_ffn2_down[l])
        w_in_b = w_in[l].astype(BF16)
        wf = jnp.pad(w_in_b[:, f0:u0], ((0, 0), (0, LANES - N_HEADS)))
        bf = jnp.pad(b_forget[l], (0, LANES - N_HEADS)).reshape(1, LANES)
        s5w = _s5_weights(ssm_a_re[l], ssm_a_im[l], ssm_log_dt[l], ssm_b_re[l], ssm_b_im[l],
                          ssm_c_re[l], ssm_c_im[l])
        in_w = (row(norm_ffn1[l]), *ffn1, row(norm_mix[l]), w_in_b[:, :f0], wf, bf, w_in_b[:, u0:ga0],
                *s5w, row(ssm_d[l]))
        mix_w = (row(norm_mix[l]), w_in_b[:, ga0:], w_attn_out[l].astype(BF16),
                 w_glu_a[l].astype(BF16), w_glu_b[l].astype(BF16), w_o[l].astype(BF16),
                 row(norm_ffn2[l]), *ffn2, row(norm_final))

        zeros = jnp.zeros((batch, n_state), F32)
        x1, q, k, v, lf, ys, hr, hi = _ffn1_mixer_in(xp, zeros, zeros, in_w,
                                                      nb=SUBLANES, tt=TOKEN_BLOCK // SUBLANES)
        attn = _attn_prompt(q, k, v, lf, tq=ATTN_BLOCK)
        xp = _mix_ffn2(x1.reshape(batch * seq, d), attn.reshape(batch * seq, ATTN_W),
                       ys.reshape(batch * seq, ssm_w), mix_w, tm=TOKEN_BLOCK, last_layer=last
                       ).reshape(batch, seq, d)
        outs["kp"].append(heads(k))
        outs["vp"].append(heads(v))
        outs["fp"].append(lf[..., :N_HEADS])
        outs["hrp"].append(hr.reshape(batch, n_groups, -1))
        outs["hip"].append(hi.reshape(batch, n_groups, -1))

        x1, q, k, v, lf, ys, hr, hi = _ffn1_mixer_in(
            xs, state_ssm_re[l].reshape(nseq, n_state), state_ssm_im[l].reshape(nseq, n_state), in_w,
            nb=TOKEN_BLOCK // n_new, tt=n_new)
        lf_new_t = jnp.pad(lf[..., :N_HEADS].transpose(0, 2, 1), ((0, 0), (0, 0), (0, LANES - n_new)))
        attn = _attn_sample(page_table, q, k, v, lf_new_t,
                            cache_k[l].transpose(0, 2, 3, 1).reshape(n_phys, ATTN_W, page),
                            cache_v[l].transpose(0, 2, 3, 1).reshape(n_phys, ATTN_W, page),
                            cache_logf[l].transpose(0, 2, 1))
        xs = _mix_ffn2(x1.reshape(nseq * n_new, d), attn.reshape(nseq * n_new, ATTN_W),
                       ys.reshape(nseq * n_new, ssm_w), mix_w, tm=TOKEN_BLOCK, last_layer=last
                       ).reshape(nseq, n_new, d)
        outs["ks"].append(heads(k))
        outs["vs"].append(heads(v))
        outs["fs"].append(lf[..., :N_HEADS])
        outs["hrs"].append(hr.reshape(nseq, n_groups, -1))
        outs["his"].append(hi.reshape(nseq, n_groups, -1))

    stack = lambda name: jnp.stack(outs[name])
    return (xp, xs, stack("kp"), stack("vp"), stack("fp"), stack("ks"), stack("vs"), stack("fs"),
            stack("hrp"), stack("hip"), stack("hrs"), stack("his"))
```

```python
import functools
import math

import jax
import jax.numpy as jnp
from jax import lax
from jax.experimental import pallas as pl
from jax.experimental.pallas import tpu as pltpu

F32 = jnp.float32
BF16 = jnp.bfloat16

N_HEADS = 8
HEAD_DIM = 64
ATTN_W = N_HEADS * HEAD_DIM
SSM_GROUP = 16
STATE_DIM = 64
EPS = 1e-6
LOG2E = math.log2(math.e)
LANES = 128
SUBLANES = 8
MXU_DIM = 256
V7X_VMEM_BYTES = 64 * 1024 * 1024
VMEM_CAP_BYTES = V7X_VMEM_BYTES - 8 * 1024 * 1024
NEG_BIG = -1e30
PAGES_PER_STEP = 16

FFN_CHUNK = MXU_DIM
TOKEN_BLOCK = 512
MIX_BLOCK = 512
ATTN_BLOCK = 256
VT_ROWS = HEAD_DIM + 16
SCAN_LANES = 512


def _vmem_limit(estimate_bytes):
    return int(min(VMEM_CAP_BYTES, max(32 * 1024 * 1024, estimate_bytes * 5 // 4)))


def _resident(shape):
    nd = len(shape)
    return pl.BlockSpec(shape, lambda *_: (0,) * nd, pipeline_mode=pl.Buffered(1))


def _nbytes(arrays):
    return sum(a.size * a.dtype.itemsize for a in arrays)


def _rmsnorm(x, g):
    return x * lax.rsqrt(jnp.mean(x * x, axis=-1, keepdims=True) + EPS) * g


def _swiglu_half(x, g_ref, wg_ref, wu_ref, wd_ref):
    h = _rmsnorm(x, g_ref[...]).astype(BF16)
    acc = jnp.zeros(x.shape, F32)
    for c0 in range(0, wg_ref.shape[1], FFN_CHUNK):
        cs = slice(c0, c0 + FFN_CHUNK)
        gate = jnp.dot(h, wg_ref[:, cs], preferred_element_type=F32)
        up = jnp.dot(h, wu_ref[:, cs], preferred_element_type=F32)
        a = (gate * jax.nn.sigmoid(gate) * up).astype(BF16)
        acc = acc + jnp.dot(a, wd_ref[cs, :], preferred_element_type=F32)
    return x + 0.5 * acc


def _log_sigmoid(x):
    return jnp.minimum(x, 0.0) - jnp.log1p(jnp.exp(-jnp.abs(x)))


def _split3(x):
    rnd = lambda a: a.astype(BF16).astype(F32)
    hi = rnd(x)
    mid = rnd(x - hi)
    lo = rnd(x - hi - mid)
    return hi, mid, lo


def _s5_discretize_kernel(a_re_ref, a_im_ref, dt_ref, bt_re_ref, bt_im_ref,
                          abar_re_ref, abar_im_ref, bbar_re_ref, bbar_im_ref):
    a_re, a_im, dt = a_re_ref[...], a_im_ref[...], dt_ref[...]
    mag = jnp.exp(a_re * dt)
    abar_re = mag * jnp.cos(a_im * dt)
    abar_im = mag * jnp.sin(a_im * dt)
    den = a_re * a_re + a_im * a_im
    num_re = abar_re - 1.0
    coef_re = (num_re * a_re + abar_im * a_im) / den
    coef_im = (abar_im * a_re - num_re * a_im) / den
    abar_re_ref[...] = abar_re
    abar_im_ref[...] = abar_im
    bt_re, bt_im = bt_re_ref[...], bt_im_ref[...]
    bbar_re_ref[...] = coef_re * bt_re - coef_im * bt_im
    bbar_im_ref[...] = coef_re * bt_im + coef_im * bt_re


def _s5_weights(a_re, a_im, log_dt, b_re, b_im, c_re, c_im):
    g, p = a_re.shape
    c = b_re.shape[2]
    dt = jnp.broadcast_to(jnp.exp(log_dt)[:, None, None], (g, 1, p))
    gp = jax.ShapeDtypeStruct((g, 1, p), F32)
    gcp = jax.ShapeDtypeStruct((g, c, p), F32)
    abar_re, abar_im, bbar_re, bbar_im = pl.pallas_call(
        _s5_discretize_kernel, out_shape=[gp, gp, gcp, gcp], name="s5_discretize",
    )(a_re[:, None, :], a_im[:, None, :], dt, b_re.transpose(0, 2, 1), b_im.transpose(0, 2, 1))
    eye = jnp.eye(g, dtype=F32)

    def in_tiles(bbar_t):
        full = jnp.einsum('gcp,gh->gchp', bbar_t, eye).reshape(g * c, g * p)
        n_tiles = g * p // MXU_DIM
        per_lane_tile = LANES * n_tiles // (g * c)
        return jnp.stack([full[LANES * (n // per_lane_tile):LANES * (n // per_lane_tile + 1),
                               MXU_DIM * n:MXU_DIM * (n + 1)] for n in range(n_tiles)]).astype(BF16)

    def out_tiles(c_mat):
        full = jnp.einsum('gcp,gh->gphc', c_mat, eye).reshape(g * p, g * c)
        n_tiles = g * c // LANES
        k = g * p // n_tiles
        return jnp.stack([full[k * j:k * (j + 1), LANES * j:LANES * (j + 1)]
                          for j in range(n_tiles)]).astype(BF16)

    return (in_tiles(bbar_re), in_tiles(bbar_im), out_tiles(c_re), out_tiles(-c_im),
            abar_re.reshape(1, g * p), abar_im.reshape(1, g * p))


def _ffn1_mixer_in_kernel(x_ref, h0re_ref, h0im_ref, g1_ref, wg_ref, wu_ref, wd_ref, g2_ref, wqkv_ref, wf_ref,
                          bf_ref, wus_ref, bre_ref, bim_ref, cre_ref, cimn_ref, are_ref, aim_ref, dskip_ref,
                          x1_ref, q_ref, k_ref, v_ref, lf_ref, y_ref, hTre_ref, hTim_ref,
                          sre_ref, sim_ref):
    nb, tt, d = x_ref.shape
    tm = nb * tt
    x1 = _swiglu_half(x_ref[...].reshape(tm, d), g1_ref, wg_ref, wu_ref, wd_ref)
    x1_ref[...] = x1.reshape(nb, tt, d)
    h = _rmsnorm(x1, g2_ref[...]).astype(BF16)
    zqkv = jnp.dot(h, wqkv_ref[...], preferred_element_type=F32)
    q_ref[...] = (zqkv[:, :ATTN_W] * (LOG2E * HEAD_DIM ** -0.5)).astype(q_ref.dtype).reshape(nb, tt, ATTN_W)
    k_ref[...] = zqkv[:, ATTN_W:2 * ATTN_W].reshape(nb, tt, ATTN_W)
    v_ref[...] = zqkv[:, 2 * ATTN_W:].reshape(nb, tt, ATTN_W)
    zf = jnp.dot(h, wf_ref[...], preferred_element_type=F32)
    lf_ref[...] = _log_sigmoid(zf + bf_ref[...]).reshape(nb, tt, LANES)

    u = jnp.dot(h, wus_ref[...], preferred_element_type=F32)
    i0 = lax.broadcasted_iota(jnp.int32, (tm, tm), 0)
    i1 = lax.broadcasted_iota(jnp.int32, (tm, tm), 1)
    to_step_major = jnp.where(i1 == (i0 % nb) * tt + i0 // nb, 1.0, 0.0).astype(BF16)
    to_seq_major = jnp.where(i0 == (i1 % nb) * tt + i1 // nb, 1.0, 0.0).astype(BF16)
    ub = jnp.dot(to_step_major, u.astype(BF16), preferred_element_type=F32).astype(BF16)
    n_in = bre_ref.shape[0]
    lane_tiles = ub.shape[1] // LANES
    for n in range(n_in):
        j = n * lane_tiles // n_in
        ut = ub[:, j * LANES:(j + 1) * LANES]
        sre_ref[:, n * MXU_DIM:(n + 1) * MXU_DIM] = jnp.dot(ut, bre_ref[n], preferred_element_type=F32)
        sim_ref[:, n * MXU_DIM:(n + 1) * MXU_DIM] = jnp.dot(ut, bim_ref[n], preferred_element_type=F32)

    @pl.when(pl.program_id(1) == 0)
    def _():
        hTre_ref[...] = h0re_ref[...]
        hTim_ref[...] = h0im_ref[...]

    n_state = sre_ref.shape[1]
    for c0 in range(0, n_state, SCAN_LANES):
        cs = slice(c0, c0 + SCAN_LANES)
        ar = jnp.broadcast_to(are_ref[:, cs], (SUBLANES, SCAN_LANES))
        ai = jnp.broadcast_to(aim_ref[:, cs], (SUBLANES, SCAN_LANES))
        for a in range(nb // SUBLANES):
            seqs = slice(a * SUBLANES, (a + 1) * SUBLANES)

            def step(t, carry, a=a, cs=cs, ar=ar, ai=ai):
                sr, si = carry
                rows = pl.ds(pl.multiple_of(t * nb + a * SUBLANES, SUBLANES), SUBLANES)
                nr = ar * sr - ai * si + sre_ref[rows, cs]
                ni = ar * si + ai * sr + sim_ref[rows, cs]
                sre_ref[rows, cs] = nr
                sim_ref[rows, cs] = ni
                return nr, ni

            sr, si = lax.fori_loop(0, tt, step, (hTre_ref[seqs, cs], hTim_ref[seqs, cs]))
            hTre_ref[seqs, cs] = sr
            hTim_ref[seqs, cs] = si

    n_out = cre_ref.shape[0]
    kw = n_state // n_out
    ys = []
    for j in range(n_out):
        hr = sre_ref[:, j * kw:(j + 1) * kw].astype(BF16)
        hi = sim_ref[:, j * kw:(j + 1) * kw].astype(BF16)
        ys.append(jnp.dot(hr, cre_ref[j], preferred_element_type=F32)
                  + jnp.dot(hi, cimn_ref[j], preferred_element_type=F32))
    y = sum(jnp.dot(to_seq_major, piece.astype(BF16), preferred_element_type=F32)
            for piece in _split3(jnp.concatenate(ys, axis=1)))
    y_ref[...] = (y + dskip_ref[...] * u).reshape(nb, tt, y.shape[1])


def _ffn1_mixer_in(x, h0_re, h0_im, weights, *, nb, tt):
    batch, seq, d = x.shape
    n_state = h0_re.shape[1]
    ssm_w = weights[8].shape[1]
    tm = nb * tt
    tok = lambda w: pl.BlockSpec((nb, tt, w), lambda i, j: (i, j, 0))
    state = pl.BlockSpec((nb, n_state), lambda i, j: (i, 0))
    widths = (d, ATTN_W, ATTN_W, ATTN_W, LANES, ssm_w)
    dtypes = (F32, BF16, F32, F32, F32, F32)
    est = (_nbytes(weights) + 2 * tm * sum(w * jnp.dtype(t).itemsize for w, t in zip(widths, dtypes))
           + 2 * tm * d * 4 + 2 * tm * n_state * 4 + 8 * nb * n_state * 4
           + tm * (3 * d * 4 + 2 * FFN_CHUNK * 4 + 3 * ATTN_W * 4 + 2 * ssm_w * 4))
    return pl.pallas_call(
        _ffn1_mixer_in_kernel,
        grid=(batch // nb, seq // tt),
        in_specs=[tok(d), state, state] + [_resident(w.shape) for w in weights],
        out_specs=[tok(w) for w in widths] + [state, state],
        out_shape=[jax.ShapeDtypeStruct((batch, seq, w), t) for w, t in zip(widths, dtypes)]
                  + [jax.ShapeDtypeStruct((batch, n_state), F32)] * 2,
        scratch_shapes=[pltpu.VMEM((tm, n_state), F32)] * 2,
        compiler_params=pltpu.CompilerParams(dimension_semantics=("arbitrary", "arbitrary"),
                                             vmem_limit_bytes=_vmem_limit(est)),
        name="ffn1_mixer_in",
    )(x, h0_re, h0_im, *weights)


def _attn_prompt_kernel(q_ref, k_ref, v_ref, lf_ref, o_ref, c_ref, kaug_ref, vt_ref, *, tq):
    seq = q_ref.shape[0]
    nq = seq // tq
    r = lax.broadcasted_iota(jnp.int32, (LANES, LANES), 0)
    cc = lax.broadcasted_iota(jnp.int32, (LANES, LANES), 1)
    tri = jnp.where(cc <= r, 1.0, 0.0).astype(F32)
    carry = jnp.zeros((1, LANES), F32)
    for blk in range(seq // LANES):
        pieces = _split3(lf_ref[blk * LANES:(blk + 1) * LANES, :])
        cb = sum(jnp.dot(tri, x, preferred_element_type=F32) for x in pieces) + carry
        carry = cb[LANES - 1:LANES, :]
        c_ref[blk * LANES:(blk + 1) * LANES, :] = cb * LOG2E
    head_lane = lax.broadcasted_iota(jnp.int32, (seq, LANES), 1) < N_HEADS
    hi, mid, lo = _split3(jnp.where(head_lane, c_ref[...], 0.0))
    c_pieces = hi + pltpu.roll(mid, N_HEADS, axis=1) + pltpu.roll(lo, 2 * N_HEADS, axis=1)

    lane = lax.broadcasted_iota(jnp.int32, (tq, LANES), 1)
    first = lane < HEAD_DIM
    key_pos = lax.broadcasted_iota(jnp.int32, (tq, tq), 0)
    qry_pos = lax.broadcasted_iota(jnp.int32, (tq, tq), 1)
    src = lax.broadcasted_iota(jnp.int32, (LANES, ATTN_W), 0)
    dst = lax.broadcasted_iota(jnp.int32, (LANES, ATTN_W), 1) % LANES
    hp_of = lax.broadcasted_iota(jnp.int32, (LANES, ATTN_W), 1) // LANES
    sel = jnp.where((dst < 6) & (src == (dst % 3) * N_HEADS + 2 * hp_of + dst // 3), 1.0, 0.0).astype(F32)
    bias = jnp.dot(c_pieces, sel, preferred_element_type=F32).astype(BF16)

    pairs = range(N_HEADS // 2)
    cols = [slice(hp * LANES, (hp + 1) * LANES) for hp in pairs]
    ones_row = jnp.where(lax.broadcasted_iota(jnp.int32, (VT_ROWS - HEAD_DIM, tq), 0) == 0, 1.0, 0.0).astype(BF16)
    for hp in pairs:
        kaug_ref[hp, :, :LANES] = k_ref[:, cols[hp]].astype(BF16)
        kaug_ref[hp, :, LANES:] = bias[:, cols[hp]]
        for j in range(nq):
            vt = v_ref[j * tq:(j + 1) * tq, cols[hp]].T.astype(BF16)
            for e in range(2):
                vt_ref[2 * hp + e, j, :HEAD_DIM] = vt[e * HEAD_DIM:(e + 1) * HEAD_DIM]
                vt_ref[2 * hp + e, j, HEAD_DIM:] = ones_row

    def q_block(qi, n_pairs, odd):
        r0 = pl.multiple_of(qi * tq, tq)
        qaug = []
        for hp in pairs:
            q = q_ref[pl.ds(r0, tq), cols[hp]]
            for e in range(2):
                qe = jnp.where(first, q, 0) if e == 0 else jnp.where(first, 0, q)
                minus_one = jnp.where((lane >= 3 * e) & (lane < 3 * e + 3), -1.0, 0.0).astype(BF16)
                qaug.append(jnp.concatenate([qe, minus_one], axis=1))

        def step(blocks, states):
            scores = []
            for hp in pairs:
                kbs = [kaug_ref[hp, pl.ds(pl.multiple_of(j * tq, tq), tq), :] for j, _ in blocks]
                for e in range(2):
                    ss = [lax.dot_general(kb, qaug[2 * hp + e], (((1,), (1,)), ((), ())),
                                          preferred_element_type=F32) for kb in kbs]
                    scores.append([jnp.where(key_pos <= qry_pos, s, NEG_BIG) if masked else s
                                   for s, (_, masked) in zip(ss, blocks)])
            probs, stats = [], []
            for h, ss in enumerate(scores):
                m = states[h][0]
                m_new = m
                for s in ss:
                    m_new = jnp.maximum(m_new, jnp.max(s, axis=0, keepdims=True))
                stats.append((m_new, jnp.exp2(m - m_new)))
                probs.append([jnp.exp2(s - m_new).astype(BF16) for s in ss])
            new = []
            for h, ps in enumerate(probs):
                m_new, alpha = stats[h]
                acc = alpha * states[h][1]
                for p, (j, _) in zip(ps, blocks):
                    acc = acc + jnp.dot(vt_ref[h, j], p, preferred_element_type=F32)
                new.append((m_new, acc))
            return tuple(new)

        init = (jnp.full((1, tq), NEG_BIG, F32), jnp.zeros((VT_ROWS, tq), F32))
        states = lax.fori_loop(0, n_pairs, lambda i, st: step([(2 * i, False), (2 * i + 1, False)], st),
                               (init,) * N_HEADS)
        states = step([(qi - 1, False), (qi, True)] if odd else [(qi, True)], states)
        for hp in pairs:
            ot = jnp.concatenate([acc[:HEAD_DIM] / acc[HEAD_DIM:HEAD_DIM + 1]
                                  for _, acc in states[2 * hp:2 * hp + 2]], axis=0)
            o_ref[pl.ds(r0, tq), cols[hp]] = ot.T.astype(o_ref.dtype)

    def q_block_pair(i, _):
        q_block(2 * i, i, False)
        q_block(2 * i + 1, i, True)
        return 0

    lax.fori_loop(0, nq // 2, q_block_pair, 0)


def _attn_prompt(q, k, v, lf, *, tq):
    batch, seq, _ = q.shape
    assert seq % (2 * tq) == 0, "query blocks are processed in even/odd pairs"
    blk = lambda w: pl.BlockSpec((None, seq, w), lambda b: (b, 0, 0))
    est = (2 * seq * ATTN_W * (2 + 4 + 4 + 2) + 2 * seq * LANES * 4 + seq * LANES * 4 * 5
           + seq * 2 * LANES * 2 * 2 + 24 * tq * tq * 4)
    return pl.pallas_call(
        functools.partial(_attn_prompt_kernel, tq=tq),
        grid=(batch,),
        in_specs=[blk(ATTN_W), blk(ATTN_W), blk(ATTN_W), blk(LANES)],
        out_specs=blk(ATTN_W),
        out_shape=jax.ShapeDtypeStruct((batch, seq, ATTN_W), BF16),
        scratch_shapes=[pltpu.VMEM((seq, LANES), F32),
                        pltpu.VMEM((N_HEADS // 2, seq, 2 * LANES), BF16),
                        pltpu.VMEM((N_HEADS, seq // tq, VT_ROWS, tq), BF16)],
        compiler_params=pltpu.CompilerParams(dimension_semantics=("arbitrary",),
                                             vmem_limit_bytes=_vmem_limit(est)),
        name="attn_prompt",
    )(q, k, v, lf)


def _suffix_and_total(x):
    r = lax.broadcasted_iota(jnp.int32, (LANES, 2 * LANES), 0)
    c = lax.broadcasted_iota(jnp.int32, (LANES, 2 * LANES), 1)
    w = jnp.where((c >= LANES) | (r > c), 1.0, 0.0).astype(F32)
    both = sum(jnp.dot(p, w, preferred_element_type=F32) for p in _split3(x))
    return both[:, :LANES], both[:, LANES:]


def _attn_sample_kernel(pt_ref, q_ref, kn_ref, vn_ref, lfn_ref, *refs, n_new):
    del pt_ref
    pp = PAGES_PER_STEP
    k_refs, v_refs, lf_refs = refs[:pp], refs[pp:2 * pp], refs[2 * pp:3 * pp]
    o_ref = refs[3 * pp]
    qbd_ref, m_ref, l_ref, acc_ref, carry_ref, r_ref = refs[3 * pp + 1:]
    rows = N_HEADS * n_new
    step = pl.program_id(1)
    hd_row = lax.broadcasted_iota(jnp.int32, (rows, ATTN_W), 0) // n_new
    hd_col = lax.broadcasted_iota(jnp.int32, (rows, ATTN_W), 1) // HEAD_DIM

    @pl.when(step == 0)
    def _():
        qt = jnp.concatenate([q_ref[...]] * N_HEADS, axis=0)
        qbd_ref[...] = jnp.where(hd_row == hd_col, qt, 0.0).astype(BF16)
        m_ref[...] = jnp.full(m_ref.shape, NEG_BIG, F32)
        l_ref[...] = jnp.zeros(l_ref.shape, F32)
        acc_ref[...] = jnp.zeros(acc_ref.shape, F32)
        carry_ref[...] = jnp.zeros(carry_ref.shape, F32)

    qbd = qbd_ref[...]

    def add_head_bias(s, slot):
        return jnp.concatenate(
            [s[h * n_new:(h + 1) * n_new, :] + r_ref[slot, h:h + 1, :] for h in range(N_HEADS)], axis=0)

    def online_update(s_list, pv_list):
        m_old = m_ref[...]
        m_new = m_old
        for s in s_list:
            m_new = jnp.maximum(m_new, jnp.max(s, axis=1, keepdims=True))
        alpha = jnp.exp2(m_old - m_new)
        l = alpha * l_ref[...]
        acc = alpha * acc_ref[...]
        for s, pv in zip(s_list, pv_list):
            p = jnp.exp2(s - m_new)
            l = l + jnp.sum(p, axis=1, keepdims=True)
            acc = acc + pv(p.astype(BF16))
        m_ref[...] = m_new
        l_ref[...] = l
        acc_ref[...] = acc

    carry = carry_ref[...]
    for i in reversed(range(pp)):
        suffix, total = _suffix_and_total(lf_refs[i][...])
        r_ref[i] = (suffix + carry) * LOG2E
        carry = carry + total
    carry_ref[...] = carry
    nt = (((1,), (1,)), ((), ()))
    online_update(
        [add_head_bias(jnp.dot(qbd, k_refs[i][...].astype(BF16), preferred_element_type=F32), i)
         for i in range(pp)],
        [lambda p, i=i: lax.dot_general(p, v_refs[i][...].astype(BF16), nt, preferred_element_type=F32)
         for i in range(pp)])

    @pl.when(step == pl.num_programs(1) - 1)
    def _():
        suffix, total = _suffix_and_total(lfn_ref[...])
        r_ref[0] = (suffix - total) * LOG2E
        pad = jnp.zeros((LANES - n_new, ATTN_W), F32)
        kn = jnp.concatenate([kn_ref[...], pad], axis=0).astype(BF16)
        vn = jnp.concatenate([vn_ref[...], pad], axis=0).astype(BF16)
        s = add_head_bias(lax.dot_general(qbd, kn, nt, preferred_element_type=F32), 0)
        qi = lax.broadcasted_iota(jnp.int32, (rows, LANES), 0) % n_new
        kj = lax.broadcasted_iota(jnp.int32, (rows, LANES), 1)
        s = jnp.where(kj <= qi, s, NEG_BIG)
        online_update([s], [lambda p: jnp.dot(p, vn, preferred_element_type=F32)])
        o = acc_ref[...] / l_ref[...]
        o = jnp.where(hd_row == hd_col, o, 0.0)
        out = o[:n_new]
        for h in range(1, N_HEADS):
            out = out + o[h * n_new:(h + 1) * n_new]
        o_ref[...] = out.astype(o_ref.dtype)


def _attn_sample(page_table, q, k_new, v_new, lf_new_t, cache_kt, cache_vt, cache_lft):
    nseq, n_pages = page_table.shape
    n_new = q.shape[1]
    pp = PAGES_PER_STEP
    rows = N_HEADS * n_new
    page = cache_kt.shape[2]

    def paged(r, i):
        def index_map(b, s, pt):
            return (pt[b * n_pages + n_pages - pp * (s + 1) + i], 0, 0)
        return pl.BlockSpec((None, r, page), index_map)

    per_seq = lambda r, w: pl.BlockSpec((None, r, w), lambda b, s, pt: (b, 0, 0))
    in_specs = ([per_seq(n_new, ATTN_W)] * 3 + [per_seq(N_HEADS, LANES)]
                + [paged(ATTN_W, i) for i in range(pp)]
                + [paged(ATTN_W, i) for i in range(pp)]
                + [paged(N_HEADS, i) for i in range(pp)])
    est = 2 * 2 * pp * page * ATTN_W * 4 + 8 * rows * ATTN_W * 4 + 4 * pp * rows * page * 4
    return pl.pallas_call(
        functools.partial(_attn_sample_kernel, n_new=n_new),
        grid_spec=pltpu.PrefetchScalarGridSpec(
            num_scalar_prefetch=1,
            grid=(nseq, n_pages // pp),
            in_specs=in_specs,
            out_specs=per_seq(n_new, ATTN_W),
            scratch_shapes=[pltpu.VMEM((rows, ATTN_W), BF16),
                            pltpu.VMEM((rows, 1), F32),
                            pltpu.VMEM((rows, 1), F32),
                            pltpu.VMEM((rows, ATTN_W), F32),
                            pltpu.VMEM((N_HEADS, LANES), F32),
                            pltpu.VMEM((pp, N_HEADS, LANES), F32)]),
        out_shape=jax.ShapeDtypeStruct((nseq, n_new, ATTN_W), BF16),
        compiler_params=pltpu.CompilerParams(dimension_semantics=("arbitrary", "arbitrary"),
                                             vmem_limit_bytes=_vmem_limit(est)),
        name="attn_sample",
    )(page_table.reshape(-1), q.astype(F32), k_new, v_new, lf_new_t,
      *([cache_kt] * pp), *([cache_vt] * pp), *([cache_lft] * pp))


def _mix_ffn2_kernel(x1_ref, attn_ref, ys_ref, gm_ref, wgate_ref, wao_ref, wga_ref, wgb_ref,
                     wo_ref, g2_ref, wg_ref, wu_ref, wd_ref, gf_ref, y_ref, *, last_layer):
    x1 = x1_ref[...]
    d = x1.shape[1]
    h = _rmsnorm(x1, gm_ref[...]).astype(BF16)
    y_attn = jnp.dot(attn_ref[...], wao_ref[...], preferred_element_type=F32)
    mixed = jax.nn.sigmoid(jnp.dot(h, wgate_ref[:, :d], preferred_element_type=F32)) * y_attn
    z = jax.nn.gelu(ys_ref[...]).astype(BF16)
    y_glu = (jnp.dot(z, wga_ref[...], preferred_element_type=F32)
             * jax.nn.sigmoid(jnp.dot(z, wgb_ref[...], preferred_element_type=F32)))
    mixed = mixed + jax.nn.sigmoid(jnp.dot(h, wgate_ref[:, d:], preferred_element_type=F32)) * y_glu
    x2 = x1 + jnp.dot(mixed.astype(BF16), wo_ref[...], preferred_element_type=F32)
    x3 = _swiglu_half(x2, g2_ref, wg_ref, wu_ref, wd_ref)
    y_ref[...] = _rmsnorm(x3, gf_ref[...]) if last_layer else x3


def _mix_ffn2(x1, attn, ys, weights, *, tm, last_layer):
    n, d = x1.shape
    tok = lambda w: pl.BlockSpec((tm, w), lambda i: (i, 0))
    est = (_nbytes(weights) + 2 * tm * (2 * d * 4 + attn.shape[1] * 2 + ys.shape[1] * 4)
           + tm * (7 * d * 4 + 2 * FFN_CHUNK * 4))
    return pl.pallas_call(
        functools.partial(_mix_ffn2_kernel, last_layer=last_layer),
        grid=(n // tm,),
        in_specs=[tok(d), tok(attn.shape[1]), tok(ys.shape[1])] + [_resident(w.shape) for w in weights],
        out_specs=tok(d),
        out_shape=jax.ShapeDtypeStruct((n, d), F32),
        compiler_params=pltpu.CompilerParams(dimension_semantics=("arbitrary",),
                                             vmem_limit_bytes=_vmem_limit(est)),
        name="mix_ffn2",
    )(x1, attn, ys, *weights)


def kernel(x_prompt, x_sample, cache_k, cache_v, cache_logf, state_ssm_re, state_ssm_im, page_table, norm_ffn1, w_ffn1_gate, w_ffn1_up, w_ffn1_down, norm_mix, w_in, b_forget, ssm_a_re, ssm_a_im, ssm_log_dt, ssm_b_re, ssm_b_im, ssm_c_re, ssm_c_im, ssm_d, w_attn_out, w_glu_a, w_glu_b, w_o, norm_ffn2, w_ffn2_gate, w_ffn2_up, w_ffn2_down, norm_final):
    depth = norm_ffn1.shape[0]
    batch, seq, d = x_prompt.shape
    nseq, n_new, _ = x_sample.shape
    n_groups, group_w = ssm_d.shape[1:]
    ssm_w = n_groups * group_w
    n_state = n_groups * ssm_a_re.shape[2]
    f0 = 3 * ATTN_W
    u0 = f0 + N_HEADS
    ga0 = u0 + ssm_w
    n_phys, page = cache_k.shape[1:3]

    xp, xs = x_prompt, x_sample
    outs = {name: [] for name in ("kp", "vp", "fp", "ks", "vs", "fs", "hrp", "hip", "hrs", "his")}
    row = lambda a: a.reshape(1, -1)
    heads = lambda a: a.reshape(a.shape[0], a.shape[1], N_HEADS, HEAD_DIM)
    for l in range(depth):
        last = l == depth - 1
        ffn1 = (w_ffn1_gate[l].astype(BF16), w_ffn1_up[l].astype(BF16), w_ffn1_down[l].astype(BF16))
        ffn2 = (w_ffn2_gate[l].astype(BF16), w_ffn2_up[l].astype(BF16), w_ffn2_down[l].astype(BF16))
        w_in_b = w_in[l].astype(BF16)
        wf = jnp.pad(w_in_b[:, f0:u0], ((0, 0), (0, LANES - N_HEADS)))
        bf = jnp.pad(b_forget[l], (0, LANES - N_HEADS)).reshape(1, LANES)
        s5w = _s5_weights(ssm_a_re[l], ssm_a_im[l], ssm_log_dt[l], ssm_b_re[l], ssm_b_im[l],
                          ssm_c_re[l], ssm_c_im[l])
        in_w = (row(norm_ffn1[l]), *ffn1, row(norm_mix[l]), w_in_b[:, :f0], wf, bf, w_in_b[:, u0:ga0],
                *s5w, row(ssm_d[l]))
        mix_w = (row(norm_mix[l]), w_in_b[:, ga0:], w_attn_out[l].astype(BF16),
                 w_glu_a[l].astype(BF16), w_glu_b[l].astype(BF16), w_o[l].astype(BF16),
                 row(norm_ffn2[l]), *ffn2, row(norm_final))

        zeros = jnp.zeros((batch, n_state), F32)
        x1, q, k, v, lf, ys, hr, hi = _ffn1_mixer_in(xp, zeros, zeros, in_w,
                                                      nb=SUBLANES, tt=TOKEN_BLOCK // SUBLANES)
        attn = _attn_prompt(q, k, v, lf, tq=ATTN_BLOCK)
        xp = _mix_ffn2(x1.reshape(batch * seq, d), attn.reshape(batch * seq, ATTN_W),
                       ys.reshape(batch * seq, ssm_w), mix_w, tm=MIX_BLOCK, last_layer=last
                       ).reshape(batch, seq, d)
        outs["kp"].append(heads(k))
        outs["vp"].append(heads(v))
        outs["fp"].append(lf[..., :N_HEADS])
        outs["hrp"].append(hr.reshape(batch, n_groups, -1))
        outs["hip"].append(hi.reshape(batch, n_groups, -1))

        x1, q, k, v, lf, ys, hr, hi = _ffn1_mixer_in(
            xs, state_ssm_re[l].reshape(nseq, n_state), state_ssm_im[l].reshape(nseq, n_state), in_w,
            nb=TOKEN_BLOCK // n_new, tt=n_new)
        lf_new_t = jnp.pad(lf[..., :N_HEADS].transpose(0, 2, 1), ((0, 0), (0, 0), (0, LANES - n_new)))
        attn = _attn_sample(page_table, q, k, v, lf_new_t,
                            cache_k[l].transpose(0, 2, 3, 1).reshape(n_phys, ATTN_W, page),
                            cache_v[l].transpose(0, 2, 3, 1).reshape(n_phys, ATTN_W, page),
                            cache_logf[l].transpose(0, 2, 1))
        xs = _mix_ffn2(x1.reshape(nseq * n_new, d), attn.reshape(nseq * n_new, ATTN_W),
                       ys.reshape(nseq * n_new, ssm_w), mix_w, tm=MIX_BLOCK, last_layer=last
                       ).reshape(nseq, n_new, d)
        outs["ks"].append(heads(k))
        outs["vs"].append(heads(v))
        outs["fs"].append(lf[..., :N_HEADS])
        outs["hrs"].append(hr.reshape(nseq, n_groups, -1))
        outs["his"].append(hi.reshape(nseq, n_groups, -1))

    stack = lambda name: jnp.stack(outs[name])
    return (xp, xs, stack("kp"), stack("vp"), stack("fp"), stack("ks"), stack("vs"), stack("fs"),
            stack("hrp"), stack("hip"), stack("hrs"), stack("his"))
```

```python
import functools
import math

import jax
import jax.numpy as jnp
from jax import lax
from jax.experimental import pallas as pl
from jax.experimental.pallas import tpu as pltpu

F32 = jnp.float32
BF16 = jnp.bfloat16

N_HEADS = 8
HEAD_DIM = 64
ATTN_W = N_HEADS * HEAD_DIM
SSM_GROUP = 16
STATE_DIM = 64
EPS = 1e-6
LOG2E = math.log2(math.e)
LANES = 128
SUBLANES = 8
MXU_DIM = 256
V7X_VMEM_BYTES = 64 * 1024 * 1024
VMEM_CAP_BYTES = V7X_VMEM_BYTES - 8 * 1024 * 1024
NEG_BIG = -1e30
PAGES_PER_STEP = 32

FFN_CHUNK = MXU_DIM
TOKEN_BLOCK = 512
MIX_BLOCK = 512
ATTN_BLOCK = 256
VT_ROWS = HEAD_DIM + 16
SCAN_LANES = 512


def _vmem_limit(estimate_bytes):
    return int(min(VMEM_CAP_BYTES, max(32 * 1024 * 1024, estimate_bytes * 5 // 4)))


def _resident(shape):
    nd = len(shape)
    return pl.BlockSpec(shape, lambda *_: (0,) * nd, pipeline_mode=pl.Buffered(1))


def _nbytes(arrays):
    return sum(a.size * a.dtype.itemsize for a in arrays)


def _rmsnorm(x, g):
    return x * lax.rsqrt(jnp.mean(x * x, axis=-1, keepdims=True) + EPS) * g


def _swiglu_half(x, g_ref, wg_ref, wu_ref, wd_ref):
    h = _rmsnorm(x, g_ref[...]).astype(BF16)
    acc = jnp.zeros(x.shape, F32)
    for c0 in range(0, wg_ref.shape[1], FFN_CHUNK):
        cs = slice(c0, c0 + FFN_CHUNK)
        gate = jnp.dot(h, wg_ref[:, cs], preferred_element_type=F32)
        up = jnp.dot(h, wu_ref[:, cs], preferred_element_type=F32)
        a = (gate * jax.nn.sigmoid(gate) * up).astype(BF16)
        acc = acc + jnp.dot(a, wd_ref[cs, :], preferred_element_type=F32)
    return x + 0.5 * acc


def _log_sigmoid(x):
    return jnp.minimum(x, 0.0) - jnp.log1p(jnp.exp(-jnp.abs(x)))


def _split3(x):
    rnd = lambda a: a.astype(BF16).astype(F32)
    hi = rnd(x)
    mid = rnd(x - hi)
    lo = rnd(x - hi - mid)
    return hi, mid, lo


def _s5_discretize_kernel(a_re_ref, a_im_ref, dt_ref, bt_re_ref, bt_im_ref,
                          abar_re_ref, abar_im_ref, bbar_re_ref, bbar_im_ref):
    a_re, a_im, dt = a_re_ref[...], a_im_ref[...], dt_ref[...]
    mag = jnp.exp(a_re * dt)
    abar_re = mag * jnp.cos(a_im * dt)
    abar_im = mag * jnp.sin(a_im * dt)
    den = a_re * a_re + a_im * a_im
    num_re = abar_re - 1.0
    coef_re = (num_re * a_re + abar_im * a_im) / den
    coef_im = (abar_im * a_re - num_re * a_im) / den
    abar_re_ref[...] = abar_re
    abar_im_ref[...] = abar_im
    bt_re, bt_im = bt_re_ref[...], bt_im_ref[...]
    bbar_re_ref[...] = coef_re * bt_re - coef_im * bt_im
    bbar_im_ref[...] = coef_re * bt_im + coef_im * bt_re


def _s5_weights(a_re, a_im, log_dt, b_re, b_im, c_re, c_im):
    g, p = a_re.shape
    c = b_re.shape[2]
    dt = jnp.broadcast_to(jnp.exp(log_dt)[:, None, None], (g, 1, p))
    gp = jax.ShapeDtypeStruct((g, 1, p), F32)
    gcp = jax.ShapeDtypeStruct((g, c, p), F32)
    abar_re, abar_im, bbar_re, bbar_im = pl.pallas_call(
        _s5_discretize_kernel, out_shape=[gp, gp, gcp, gcp], name="s5_discretize",
    )(a_re[:, None, :], a_im[:, None, :], dt, b_re.transpose(0, 2, 1), b_im.transpose(0, 2, 1))
    eye = jnp.eye(g, dtype=F32)

    def in_tiles(bbar_t):
        full = jnp.einsum('gcp,gh->gchp', bbar_t, eye).reshape(g * c, g * p)
        n_tiles = g * p // MXU_DIM
        per_lane_tile = LANES * n_tiles // (g * c)
        return jnp.stack([full[LANES * (n // per_lane_tile):LANES * (n // per_lane_tile + 1),
                               MXU_DIM * n:MXU_DIM * (n + 1)] for n in range(n_tiles)]).astype(BF16)

    def out_tiles(c_mat):
        full = jnp.einsum('gcp,gh->gphc', c_mat, eye).reshape(g * p, g * c)
        n_tiles = g * c // LANES
        k = g * p // n_tiles
        return jnp.stack([full[k * j:k * (j + 1), LANES * j:LANES * (j + 1)]
                          for j in range(n_tiles)]).astype(BF16)

    return (in_tiles(bbar_re), in_tiles(bbar_im), out_tiles(c_re), out_tiles(-c_im),
            abar_re.reshape(1, g * p), abar_im.reshape(1, g * p))


def _ffn1_mixer_in_kernel(x_ref, h0re_ref, h0im_ref, g1_ref, wg_ref, wu_ref, wd_ref, g2_ref, wqkv_ref, wf_ref,
                          bf_ref, wus_ref, bre_ref, bim_ref, cre_ref, cimn_ref, are_ref, aim_ref, dskip_ref,
                          x1_ref, q_ref, k_ref, v_ref, lf_ref, y_ref, hTre_ref, hTim_ref,
                          sre_ref, sim_ref):
    nb, tt, d = x_ref.shape
    tm = nb * tt
    x1 = _swiglu_half(x_ref[...].reshape(tm, d), g1_ref, wg_ref, wu_ref, wd_ref)
    x1_ref[...] = x1.reshape(nb, tt, d)
    h = _rmsnorm(x1, g2_ref[...]).astype(BF16)
    zqkv = jnp.dot(h, wqkv_ref[...], preferred_element_type=F32)
    q_ref[...] = (zqkv[:, :ATTN_W] * (LOG2E * HEAD_DIM ** -0.5)).astype(q_ref.dtype).reshape(nb, tt, ATTN_W)
    k_ref[...] = zqkv[:, ATTN_W:2 * ATTN_W].reshape(nb, tt, ATTN_W)
    v_ref[...] = zqkv[:, 2 * ATTN_W:].reshape(nb, tt, ATTN_W)
    zf = jnp.dot(h, wf_ref[...], preferred_element_type=F32)
    lf_ref[...] = _log_sigmoid(zf + bf_ref[...]).reshape(nb, tt, LANES)

    u = jnp.dot(h, wus_ref[...], preferred_element_type=F32)
    i0 = lax.broadcasted_iota(jnp.int32, (tm, tm), 0)
    i1 = lax.broadcasted_iota(jnp.int32, (tm, tm), 1)
    to_step_major = jnp.where(i1 == (i0 % nb) * tt + i0 // nb, 1.0, 0.0).astype(BF16)
    to_seq_major = jnp.where(i0 == (i1 % nb) * tt + i1 // nb, 1.0, 0.0).astype(BF16)
    ub = jnp.dot(to_step_major, u.astype(BF16), preferred_element_type=F32).astype(BF16)
    n_in = bre_ref.shape[0]
    lane_tiles = ub.shape[1] // LANES
    for n in range(n_in):
        j = n * lane_tiles // n_in
        ut = ub[:, j * LANES:(j + 1) * LANES]
        sre_ref[:, n * MXU_DIM:(n + 1) * MXU_DIM] = jnp.dot(ut, bre_ref[n], preferred_element_type=F32)
        sim_ref[:, n * MXU_DIM:(n + 1) * MXU_DIM] = jnp.dot(ut, bim_ref[n], preferred_element_type=F32)

    @pl.when(pl.program_id(1) == 0)
    def _():
        hTre_ref[...] = h0re_ref[...]
        hTim_ref[...] = h0im_ref[...]

    n_state = sre_ref.shape[1]
    for c0 in range(0, n_state, SCAN_LANES):
        cs = slice(c0, c0 + SCAN_LANES)
        ar = jnp.broadcast_to(are_ref[:, cs], (SUBLANES, SCAN_LANES))
        ai = jnp.broadcast_to(aim_ref[:, cs], (SUBLANES, SCAN_LANES))
        for a in range(nb // SUBLANES):
            seqs = slice(a * SUBLANES, (a + 1) * SUBLANES)

            def step(t, carry, a=a, cs=cs, ar=ar, ai=ai):
                sr, si = carry
                rows = pl.ds(pl.multiple_of(t * nb + a * SUBLANES, SUBLANES), SUBLANES)
                nr = ar * sr - ai * si + sre_ref[rows, cs]
                ni = ar * si + ai * sr + sim_ref[rows, cs]
                sre_ref[rows, cs] = nr
                sim_ref[rows, cs] = ni
                return nr, ni

            sr, si = lax.fori_loop(0, tt, step, (hTre_ref[seqs, cs], hTim_ref[seqs, cs]))
            hTre_ref[seqs, cs] = sr
            hTim_ref[seqs, cs] = si

    n_out = cre_ref.shape[0]
    kw = n_state // n_out
    ys = []
    for j in range(n_out):
        hr = sre_ref[:, j * kw:(j + 1) * kw].astype(BF16)
        hi = sim_ref[:, j * kw:(j + 1) * kw].astype(BF16)
        ys.append(jnp.dot(hr, cre_ref[j], preferred_element_type=F32)
                  + jnp.dot(hi, cimn_ref[j], preferred_element_type=F32))
    y = sum(jnp.dot(to_seq_major, piece.astype(BF16), preferred_element_type=F32)
            for piece in _split3(jnp.concatenate(ys, axis=1))[:2])
    y_ref[...] = (y + dskip_ref[...] * u).reshape(nb, tt, y.shape[1])


def _ffn1_mixer_in(x, h0_re, h0_im, weights, *, nb, tt):
    batch, seq, d = x.shape
    n_state = h0_re.shape[1]
    ssm_w = weights[8].shape[1]
    tm = nb * tt
    tok = lambda w: pl.BlockSpec((nb, tt, w), lambda i, j: (i, j, 0))
    state = pl.BlockSpec((nb, n_state), lambda i, j: (i, 0))
    widths = (d, ATTN_W, ATTN_W, ATTN_W, LANES, ssm_w)
    dtypes = (F32, BF16, F32, F32, F32, F32)
    est = (_nbytes(weights) + 2 * tm * sum(w * jnp.dtype(t).itemsize for w, t in zip(widths, dtypes))
           + 2 * tm * d * 4 + 2 * tm * n_state * 4 + 8 * nb * n_state * 4
           + tm * (3 * d * 4 + 2 * FFN_CHUNK * 4 + 3 * ATTN_W * 4 + 2 * ssm_w * 4))
    return pl.pallas_call(
        _ffn1_mixer_in_kernel,
        grid=(batch // nb, seq // tt),
        in_specs=[tok(d), state, state] + [_resident(w.shape) for w in weights],
        out_specs=[tok(w) for w in widths] + [state, state],
        out_shape=[jax.ShapeDtypeStruct((batch, seq, w), t) for w, t in zip(widths, dtypes)]
                  + [jax.ShapeDtypeStruct((batch, n_state), F32)] * 2,
        scratch_shapes=[pltpu.VMEM((tm, n_state), F32)] * 2,
        compiler_params=pltpu.CompilerParams(dimension_semantics=("arbitrary", "arbitrary"),
                                             vmem_limit_bytes=_vmem_limit(est)),
        name="ffn1_mixer_in",
    )(x, h0_re, h0_im, *weights)


def _attn_prompt_kernel(q_ref, k_ref, v_ref, lf_ref, o_ref, c_ref, kaug_ref, vt_ref, *, tq):
    seq = q_ref.shape[0]
    nq = seq // tq
    r = lax.broadcasted_iota(jnp.int32, (LANES, LANES), 0)
    cc = lax.broadcasted_iota(jnp.int32, (LANES, LANES), 1)
    tri = jnp.where(cc <= r, 1.0, 0.0).astype(F32)
    carry = jnp.zeros((1, LANES), F32)
    for blk in range(seq // LANES):
        pieces = _split3(lf_ref[blk * LANES:(blk + 1) * LANES, :])
        cb = sum(jnp.dot(tri, x, preferred_element_type=F32) for x in pieces) + carry
        carry = cb[LANES - 1:LANES, :]
        c_ref[blk * LANES:(blk + 1) * LANES, :] = cb * LOG2E
    head_lane = lax.broadcasted_iota(jnp.int32, (seq, LANES), 1) < N_HEADS
    hi, mid, lo = _split3(jnp.where(head_lane, c_ref[...], 0.0))
    c_pieces = hi + pltpu.roll(mid, N_HEADS, axis=1) + pltpu.roll(lo, 2 * N_HEADS, axis=1)

    lane = lax.broadcasted_iota(jnp.int32, (tq, LANES), 1)
    first = lane < HEAD_DIM
    key_pos = lax.broadcasted_iota(jnp.int32, (tq, tq), 0)
    qry_pos = lax.broadcasted_iota(jnp.int32, (tq, tq), 1)
    src = lax.broadcasted_iota(jnp.int32, (LANES, ATTN_W), 0)
    dst = lax.broadcasted_iota(jnp.int32, (LANES, ATTN_W), 1) % LANES
    hp_of = lax.broadcasted_iota(jnp.int32, (LANES, ATTN_W), 1) // LANES
    sel = jnp.where((dst < 6) & (src == (dst % 3) * N_HEADS + 2 * hp_of + dst // 3), 1.0, 0.0).astype(F32)
    bias = jnp.dot(c_pieces, sel, preferred_element_type=F32).astype(BF16)

    pairs = range(N_HEADS // 2)
    cols = [slice(hp * LANES, (hp + 1) * LANES) for hp in pairs]
    ones_row = jnp.where(lax.broadcasted_iota(jnp.int32, (VT_ROWS - HEAD_DIM, tq), 0) == 0, 1.0, 0.0).astype(BF16)
    for hp in pairs:
        kaug_ref[hp, :, :LANES] = k_ref[:, cols[hp]].astype(BF16)
        kaug_ref[hp, :, LANES:] = bias[:, cols[hp]]
        for j in range(nq):
            vt = v_ref[j * tq:(j + 1) * tq, cols[hp]].T.astype(BF16)
            for e in range(2):
                vt_ref[2 * hp + e, j, :HEAD_DIM] = vt[e * HEAD_DIM:(e + 1) * HEAD_DIM]
                vt_ref[2 * hp + e, j, HEAD_DIM:] = ones_row

    def q_block(qi, n_pairs, odd):
        r0 = pl.multiple_of(qi * tq, tq)
        qaug = []
        for hp in pairs:
            q = q_ref[pl.ds(r0, tq), cols[hp]]
            for e in range(2):
                qe = jnp.where(first, q, 0) if e == 0 else jnp.where(first, 0, q)
                minus_one = jnp.where((lane >= 3 * e) & (lane < 3 * e + 3), -1.0, 0.0).astype(BF16)
                qaug.append(jnp.concatenate([qe, minus_one], axis=1))

        def step(blocks, states):
            scores = []
            for hp in pairs:
                kbs = [kaug_ref[hp, pl.ds(pl.multiple_of(j * tq, tq), tq), :] for j, _ in blocks]
                for e in range(2):
                    ss = [lax.dot_general(kb, qaug[2 * hp + e], (((1,), (1,)), ((), ())),
                                          preferred_element_type=F32) for kb in kbs]
                    scores.append([jnp.where(key_pos <= qry_pos, s, NEG_BIG) if masked else s
                                   for s, (_, masked) in zip(ss, blocks)])
            probs, stats = [], []
            for h, ss in enumerate(scores):
                m = states[h][0]
                m_new = m
                for s in ss:
                    m_new = jnp.maximum(m_new, jnp.max(s, axis=0, keepdims=True))
                stats.append((m_new, jnp.exp2(m - m_new)))
                probs.append([jnp.exp2(s - m_new).astype(BF16) for s in ss])
            new = []
            for h, ps in enumerate(probs):
                m_new, alpha = stats[h]
                acc = alpha * states[h][1]
                for p, (j, _) in zip(ps, blocks):
                    acc = acc + jnp.dot(vt_ref[h, j], p, preferred_element_type=F32)
                new.append((m_new, acc))
            return tuple(new)

        init = (jnp.full((1, tq), NEG_BIG, F32), jnp.zeros((VT_ROWS, tq), F32))
        states = lax.fori_loop(0, n_pairs, lambda i, st: step([(2 * i, False), (2 * i + 1, False)], st),
                               (init,) * N_HEADS)
        states = step([(qi - 1, False), (qi, True)] if odd else [(qi, True)], states)
        for hp in pairs:
            ot = jnp.concatenate([acc[:HEAD_DIM] / acc[HEAD_DIM:HEAD_DIM + 1]
                                  for _, acc in states[2 * hp:2 * hp + 2]], axis=0)
            o_ref[pl.ds(r0, tq), cols[hp]] = ot.T.astype(o_ref.dtype)

    def q_block_pair(i, _):
        q_block(2 * i, i, False)
        q_block(2 * i + 1, i, True)
        return 0

    lax.fori_loop(0, nq // 2, q_block_pair, 0)


def _attn_prompt(q, k, v, lf, *, tq):
    batch, seq, _ = q.shape
    assert seq % (2 * tq) == 0, "query blocks are processed in even/odd pairs"
    blk = lambda w: pl.BlockSpec((None, seq, w), lambda b: (b, 0, 0))
    est = (2 * seq * ATTN_W * (2 + 4 + 4 + 2) + 2 * seq * LANES * 4 + seq * LANES * 4 * 5
           + seq * 2 * LANES * 2 * 2 + 24 * tq * tq * 4)
    return pl.pallas_call(
        functools.partial(_attn_prompt_kernel, tq=tq),
        grid=(batch,),
        in_specs=[blk(ATTN_W), blk(ATTN_W), blk(ATTN_W), blk(LANES)],
        out_specs=blk(ATTN_W),
        out_shape=jax.ShapeDtypeStruct((batch, seq, ATTN_W), BF16),
        scratch_shapes=[pltpu.VMEM((seq, LANES), F32),
                        pltpu.VMEM((N_HEADS // 2, seq, 2 * LANES), BF16),
                        pltpu.VMEM((N_HEADS, seq // tq, VT_ROWS, tq), BF16)],
        compiler_params=pltpu.CompilerParams(dimension_semantics=("arbitrary",),
                                             vmem_limit_bytes=_vmem_limit(est)),
        name="attn_prompt",
    )(q, k, v, lf)


def _suffix_and_total(x):
    r = lax.broadcasted_iota(jnp.int32, (LANES, 2 * LANES), 0)
    c = lax.broadcasted_iota(jnp.int32, (LANES, 2 * LANES), 1)
    w = jnp.where((c >= LANES) | (r > c), 1.0, 0.0).astype(F32)
    both = jnp.dot(jnp.concatenate(_split3(x), axis=0), w, preferred_element_type=F32)
    n = x.shape[0]
    both = both[:n] + both[n:2 * n] + both[2 * n:]
    return both[:, :LANES], both[:, LANES:]


def _attn_sample_kernel(pt_ref, q_ref, kn_ref, vn_ref, lfn_ref, *refs, n_new):
    del pt_ref
    pp = PAGES_PER_STEP
    k_refs, v_refs, lf_refs = refs[:pp], refs[pp:2 * pp], refs[2 * pp:3 * pp]
    o_ref = refs[3 * pp]
    qbd_ref, m_ref, l_ref, acc_ref, carry_ref, r_ref = refs[3 * pp + 1:]
    rows = N_HEADS * n_new
    step = pl.program_id(1)
    hd_row = lax.broadcasted_iota(jnp.int32, (rows, ATTN_W), 0) // n_new
    hd_col = lax.broadcasted_iota(jnp.int32, (rows, ATTN_W), 1) // HEAD_DIM

    @pl.when(step == 0)
    def _():
        qt = jnp.concatenate([q_ref[...]] * N_HEADS, axis=0)
        qbd_ref[...] = jnp.where(hd_row == hd_col, qt, 0.0).astype(BF16)
        m_ref[...] = jnp.full(m_ref.shape, NEG_BIG, F32)
        l_ref[...] = jnp.zeros(l_ref.shape, F32)
        acc_ref[...] = jnp.zeros(acc_ref.shape, F32)
        carry_ref[...] = jnp.zeros(carry_ref.shape, F32)

    qbd = qbd_ref[...]

    def add_head_bias(s, slot):
        return jnp.concatenate(
            [s[h * n_new:(h + 1) * n_new, :] + r_ref[slot, h:h + 1, :] for h in range(N_HEADS)], axis=0)

    def online_update(s_list, pv_list):
        m_old = m_ref[...]
        m_new = m_old
        for s in s_list:
            m_new = jnp.maximum(m_new, jnp.max(s, axis=1, keepdims=True))
        alpha = jnp.exp2(m_old - m_new)
        l = alpha * l_ref[...]
        acc = alpha * acc_ref[...]
        for s, pv in zip(s_list, pv_list):
            p = jnp.exp2(s - m_new)
            l = l + jnp.sum(p, axis=1, keepdims=True)
            acc = acc + pv(p.astype(BF16))
        m_ref[...] = m_new
        l_ref[...] = l
        acc_ref[...] = acc

    carry = carry_ref[...]
    suffix, total = _suffix_and_total(jnp.concatenate([lf_refs[i][...] for i in range(pp)], axis=0))
    for i in reversed(range(pp)):
        r_ref[i] = (suffix[i * N_HEADS:(i + 1) * N_HEADS] + carry) * LOG2E
        carry = carry + total[i * N_HEADS:(i + 1) * N_HEADS]
    carry_ref[...] = carry
    nt = (((1,), (1,)), ((), ()))
    online_update(
        [add_head_bias(jnp.dot(qbd, k_refs[i][...].astype(BF16), preferred_element_type=F32), i)
         for i in range(pp)],
        [lambda p, i=i: lax.dot_general(p, v_refs[i][...].astype(BF16), nt, preferred_element_type=F32)
         for i in range(pp)])

    @pl.when(step == pl.num_programs(1) - 1)
    def _():
        suffix, total = _suffix_and_total(lfn_ref[...])
        r_ref[0] = (suffix - total) * LOG2E
        pad = jnp.zeros((LANES - n_new, ATTN_W), F32)
        kn = jnp.concatenate([kn_ref[...], pad], axis=0).astype(BF16)
        vn = jnp.concatenate([vn_ref[...], pad], axis=0).astype(BF16)
        s = add_head_bias(lax.dot_general(qbd, kn, nt, preferred_element_type=F32), 0)
        qi = lax.broadcasted_iota(jnp.int32, (rows, LANES), 0) % n_new
        kj = lax.broadcasted_iota(jnp.int32, (rows, LANES), 1)
        s = jnp.where(kj <= qi, s, NEG_BIG)
        online_update([s], [lambda p: jnp.dot(p, vn, preferred_element_type=F32)])
        o = acc_ref[...] / l_ref[...]
        o = jnp.where(hd_row == hd_col, o, 0.0)
        out = o[:n_new]
        for h in range(1, N_HEADS):
            out = out + o[h * n_new:(h + 1) * n_new]
        o_ref[...] = out.astype(o_ref.dtype)


def _attn_sample(page_table, q, k_new, v_new, lf_new_t, cache_kt, cache_vt, cache_lft):
    nseq, n_pages = page_table.shape
    n_new = q.shape[1]
    pp = PAGES_PER_STEP
    rows = N_HEADS * n_new
    page = cache_kt.shape[2]

    def paged(r, i):
        def index_map(b, s, pt):
            return (pt[b * n_pages + n_pages - pp * (s + 1) + i], 0, 0)
        return pl.BlockSpec((None, r, page), index_map)

    per_seq = lambda r, w: pl.BlockSpec((None, r, w), lambda b, s, pt: (b, 0, 0))
    in_specs = ([per_seq(n_new, ATTN_W)] * 3 + [per_seq(N_HEADS, LANES)]
                + [paged(ATTN_W, i) for i in range(pp)]
                + [paged(ATTN_W, i) for i in range(pp)]
                + [paged(N_HEADS, i) for i in range(pp)])
    est = 2 * 2 * pp * page * ATTN_W * 4 + 8 * rows * ATTN_W * 4 + 4 * pp * rows * page * 4
    return pl.pallas_call(
        functools.partial(_attn_sample_kernel, n_new=n_new),
        grid_spec=pltpu.PrefetchScalarGridSpec(
            num_scalar_prefetch=1,
            grid=(nseq, n_pages // pp),
            in_specs=in_specs,
            out_specs=per_seq(n_new, ATTN_W),
            scratch_shapes=[pltpu.VMEM((rows, ATTN_W), BF16),
                            pltpu.VMEM((rows, 1), F32),
                            pltpu.VMEM((rows, 1), F32),
                            pltpu.VMEM((rows, ATTN_W), F32),
                            pltpu.VMEM((N_HEADS, LANES), F32),
                            pltpu.VMEM((pp, N_HEADS, LANES), F32)]),
        out_shape=jax.ShapeDtypeStruct((nseq, n_new, ATTN_W), BF16),
        compiler_params=pltpu.CompilerParams(dimension_semantics=("arbitrary", "arbitrary"),
                                             vmem_limit_bytes=_vmem_limit(est)),
        name="attn_sample",
    )(page_table.reshape(-1), q.astype(F32), k_new, v_new, lf_new_t,
      *([cache_kt] * pp), *([cache_vt] * pp), *([cache_lft] * pp))


def _mix_ffn2_kernel(x1_ref, attn_ref, ys_ref, gm_ref, wgate_ref, wao_ref, wga_ref, wgb_ref,
                     wo_ref, g2_ref, wg_ref, wu_ref, wd_ref, gf_ref, y_ref, *, last_layer):
    x1 = x1_ref[...]
    d = x1.shape[1]
    h = _rmsnorm(x1, gm_ref[...]).astype(BF16)
    y_attn = jnp.dot(attn_ref[...], wao_ref[...], preferred_element_type=F32)
    mixed = jax.nn.sigmoid(jnp.dot(h, wgate_ref[:, :d], preferred_element_type=F32)) * y_attn
    z = jax.nn.gelu(ys_ref[...]).astype(BF16)
    y_glu = (jnp.dot(z, wga_ref[...], preferred_element_type=F32)
             * jax.nn.sigmoid(jnp.dot(z, wgb_ref[...], preferred_element_type=F32)))
    mixed = mixed + jax.nn.sigmoid(jnp.dot(h, wgate_ref[:, d:], preferred_element_type=F32)) * y_glu
    x2 = x1 + jnp.dot(mixed.astype(BF16), wo_ref[...], preferred_element_type=F32)
    x3 = _swiglu_half(x2, g2_ref, wg_ref, wu_ref, wd_ref)
    y_ref[...] = _rmsnorm(x3, gf_ref[...]) if last_layer else x3


def _mix_ffn2(x1, attn, ys, weights, *, tm, last_layer):
    n, d = x1.shape
    tok = lambda w: pl.BlockSpec((tm, w), lambda i: (i, 0))
    est = (_nbytes(weights) + 2 * tm * (2 * d * 4 + attn.shape[1] * 2 + ys.shape[1] * 4)
           + tm * (7 * d * 4 + 2 * FFN_CHUNK * 4))
    return pl.pallas_call(
        functools.partial(_mix_ffn2_kernel, last_layer=last_layer),
        grid=(n // tm,),
        in_specs=[tok(d), tok(attn.shape[1]), tok(ys.shape[1])] + [_resident(w.shape) for w in weights],
        out_specs=tok(d),
        out_shape=jax.ShapeDtypeStruct((n, d), F32),
        compiler_params=pltpu.CompilerParams(dimension_semantics=("arbitrary",),
                                             vmem_limit_bytes=_vmem_limit(est)),
        name="mix_ffn2",
    )(x1, attn, ys, *weights)


def kernel(x_prompt, x_sample, cache_k, cache_v, cache_logf, state_ssm_re, state_ssm_im, page_table, norm_ffn1, w_ffn1_gate, w_ffn1_up, w_ffn1_down, norm_mix, w_in, b_forget, ssm_a_re, ssm_a_im, ssm_log_dt, ssm_b_re, ssm_b_im, ssm_c_re, ssm_c_im, ssm_d, w_attn_out, w_glu_a, w_glu_b, w_o, norm_ffn2, w_ffn2_gate, w_ffn2_up, w_ffn2_down, norm_final):
    depth = norm_ffn1.shape[0]
    batch, seq, d = x_prompt.shape
    nseq, n_new, _ = x_sample.shape
    n_groups, group_w = ssm_d.shape[1:]
    ssm_w = n_groups * group_w
    n_state = n_groups * ssm_a_re.shape[2]
    f0 = 3 * ATTN_W
    u0 = f0 + N_HEADS
    ga0 = u0 + ssm_w
    n_phys, page = cache_k.shape[1:3]

    xp, xs = x_prompt, x_sample
    outs = {name: [] for name in ("kp", "vp", "fp", "ks", "vs", "fs", "hrp", "hip", "hrs", "his")}
    row = lambda a: a.reshape(1, -1)
    heads = lambda a: a.reshape(a.shape[0], a.shape[1], N_HEADS, HEAD_DIM)
    for l in range(depth):
        last = l == depth - 1
        ffn1 = (w_ffn1_gate[l].astype(BF16), w_ffn1_up[l].astype(BF16), w_ffn1_down[l].astype(BF16))
        ffn2 = (w_ffn2_gate[l].astype(BF16), w_ffn2_up[l].astype(BF16), w_ffn2_down[l].astype(BF16))
        w_in_b = w_in[l].astype(BF16)
        wf = jnp.pad(w_in_b[:, f0:u0], ((0, 0), (0, LANES - N_HEADS)))
        bf = jnp.pad(b_forget[l], (0, LANES - N_HEADS)).reshape(1, LANES)
        s5w = _s5_weights(ssm_a_re[l], ssm_a_im[l], ssm_log_dt[l], ssm_b_re[l], ssm_b_im[l],
                          ssm_c_re[l], ssm_c_im[l])
        in_w = (row(norm_ffn1[l]), *ffn1, row(norm_mix[l]), w_in_b[:, :f0], wf, bf, w_in_b[:, u0:ga0],
                *s5w, row(ssm_d[l]))
        mix_w = (row(norm_mix[l]), w_in_b[:, ga0:], w_attn_out[l].astype(BF16),
                 w_glu_a[l].astype(BF16), w_glu_b[l].astype(BF16), w_o[l].astype(BF16),
                 row(norm_ffn2[l]), *ffn2, row(norm_final))

        zeros = jnp.zeros((batch, n_state), F32)
        x1, q, k, v, lf, ys, hr, hi = _ffn1_mixer_in(xp, zeros, zeros, in_w,
                                                      nb=SUBLANES, tt=TOKEN_BLOCK // SUBLANES)
        attn = _attn_prompt(q, k, v, lf, tq=ATTN_BLOCK)
        xp = _mix_ffn2(x1.reshape(batch * seq, d), attn.reshape(batch * seq, ATTN_W),
                       ys.reshape(batch * seq, ssm_w), mix_w, tm=MIX_BLOCK, last_layer=last
                       ).reshape(batch, seq, d)
        outs["kp"].append(heads(k))
        outs["vp"].append(heads(v))
        outs["fp"].append(lf[..., :N_HEADS])
        outs["hrp"].append(hr.reshape(batch, n_groups, -1))
        outs["hip"].append(hi.reshape(batch, n_groups, -1))

        x1, q, k, v, lf, ys, hr, hi = _ffn1_mixer_in(
            xs, state_ssm_re[l].reshape(nseq, n_state), state_ssm_im[l].reshape(nseq, n_state), in_w,
            nb=TOKEN_BLOCK // n_new, tt=n_new)
        lf_new_t = jnp.pad(lf[..., :N_HEADS].transpose(0, 2, 1), ((0, 0), (0, 0), (0, LANES - n_new)))
        attn = _attn_sample(page_table, q, k, v, lf_new_t,
                            cache_k[l].transpose(0, 2, 3, 1).reshape(n_phys, ATTN_W, page),
                            cache_v[l].transpose(0, 2, 3, 1).reshape(n_phys, ATTN_W, page),
                            cache_logf[l].transpose(0, 2, 1))
        xs = _mix_ffn2(x1.reshape(nseq * n_new, d), attn.reshape(nseq * n_new, ATTN_W),
                       ys.reshape(nseq * n_new, ssm_w), mix_w, tm=MIX_BLOCK, last_layer=last
                       ).reshape(nseq, n_new, d)
        outs["ks"].append(heads(k))
        outs["vs"].append(heads(v))
        outs["fs"].append(lf[..., :N_HEADS])
        outs["hrs"].append(hr.reshape(nseq, n_groups, -1))
        outs["his"].append(hi.reshape(nseq, n_groups, -1))

    stack = lambda name: jnp.stack(outs[name])
    return (xp, xs, stack("kp"), stack("vp"), stack("fp"), stack("ks"), stack("vs"), stack("fs"),
            stack("hrp"), stack("hip"), stack("hrs"), stack("his"))
```
